```python
import jax, jax.numpy as jnp
from jax import lax
import numpy as np

D_MODEL = 4096
BATCH = 8
SEQ = 2048
DEPTH = 2

N_A_LAYERS = (DEPTH + 1) // 2
N_B_LAYERS = DEPTH - N_A_LAYERS
CONV_KERNEL = 31
N_HEADS = 32
QK_NOPE_DIM = 128
QK_ROPE_DIM = 64
V_HEAD_DIM = 128
Q_LORA_RANK = 1024
KV_LORA_RANK = 512
ROPE_THETA = 10000.0
Q_BLOCK = 128
SOFTMAX_SCALE = (QK_NOPE_DIM + QK_ROPE_DIM) ** -0.5
MAX_POS_OFFSET = 1024
N_EXPERTS = 32
N_GROUPS = 8
EXPERTS_PER_GROUP = N_EXPERTS // N_GROUPS
TOP_K = 2
D_EXPERT = 768
PLE_DIM = 256
ALPHA = (2 * DEPTH) ** 0.25
BETA = (8 * DEPTH) ** -0.25
LN_EPS = 1e-5
RMS_EPS = 1e-6

kernel_name = "yoco_conformer_mla_grouped_moe_deepnorm_ple"


def layer_norm(x, g, b):
    xf = x.astype(jnp.float32)
    mu = xf.mean(-1, keepdims=True)
    xc = xf - mu
    var = (xc * xc).mean(-1, keepdims=True)
    return (xc * lax.rsqrt(var + LN_EPS) * g.astype(jnp.float32) + b.astype(jnp.float32)).astype(x.dtype)


def rms_norm(x, g):
    xf = x.astype(jnp.float32)
    ms = (xf * xf).mean(-1, keepdims=True)
    return (xf * lax.rsqrt(ms + RMS_EPS) * g.astype(jnp.float32)).astype(x.dtype)


def rope_tables(positions, dtype):
    inv_freq = 1.0 / (ROPE_THETA ** (jnp.arange(0, QK_ROPE_DIM, 2, dtype=jnp.float32) / QK_ROPE_DIM))
    ang = positions.astype(jnp.float32)[..., None] * inv_freq
    return jnp.cos(ang).astype(dtype), jnp.sin(ang).astype(dtype)


def apply_rope(x, cos, sin):
    x1, x2 = jnp.split(x, 2, axis=-1)
    return jnp.concatenate([x1 * cos - x2 * sin, x1 * sin + x2 * cos], axis=-1)


def conformer_conv(x, w_in, b_in, w_dw, b_dw, ln_g, ln_b, w_out, b_out):
    h = x @ w_in + b_in
    a, g = jnp.split(h, 2, axis=-1)
    h = a * jax.nn.sigmoid(g)
    h = lax.conv_general_dilated(
        h, w_dw, window_strides=(1,), padding=((CONV_KERNEL - 1, 0),),
        dimension_numbers=("NWC", "WIO", "NWC"), feature_group_count=h.shape[-1]) + b_dw
    h = jax.nn.silu(layer_norm(h, ln_g, ln_b))
    return h @ w_out + b_out


def shared_kv(h, w_down, norm_g, w_up, cos, sin):
    c = h @ w_down
    c_kv, k_pe = c[..., :KV_LORA_RANK], c[..., KV_LORA_RANK:]
    c_kv = rms_norm(c_kv, norm_g)
    kv = jnp.einsum("bsr,rhd->bshd", c_kv, w_up)
    k_nope, v = kv[..., :QK_NOPE_DIM], kv[..., QK_NOPE_DIM:]
    k_pe = apply_rope(k_pe, cos, sin)
    return k_nope, k_pe, v


def causal_latent_attention(q_nope, q_pe, k_nope, k_pe, v):
    s_len = q_nope.shape[1]
    q_idx = jnp.arange(Q_BLOCK)
    outs = []
    for blk in range(s_len // Q_BLOCK):
        q0 = blk * Q_BLOCK
        q1 = q0 + Q_BLOCK
        s = (jnp.einsum("bqhd,bkhd->bhqk", q_nope[:, q0:q1], k_nope[:, :q1])
             + jnp.einsum("bqhr,bkr->bhqk", q_pe[:, q0:q1], k_pe[:, :q1]))
        s = s.astype(jnp.float32) * SOFTMAX_SCALE
        causal = jnp.arange(q1)[None, :] <= (q0 + q_idx)[:, None]
        s = jnp.where(causal, s, -jnp.inf)
        prob = jax.nn.softmax(s, axis=-1).astype(v.dtype)
        outs.append(jnp.einsum("bhqk,bkhd->bqhd", prob, v[:, :q1]))
    return jnp.concatenate(outs, axis=1)


def mla_layer(x, k_nope, k_pe, v, cos, sin, w_dq, q_norm_g, w_uq, w_o):
    b, s, _ = x.shape
    cq = rms_norm(x @ w_dq, q_norm_g)
    q = jnp.einsum("bsr,rhd->bshd", cq, w_uq)
    q_nope = q[..., :QK_NOPE_DIM]
    q_pe = apply_rope(q[..., QK_NOPE_DIM:], cos[:, :, None, :], sin[:, :, None, :])
    o = causal_latent_attention(q_nope, q_pe, k_nope, k_pe, v)
    return o.reshape(b, s, N_HEADS * V_HEAD_DIM) @ w_o


def grouped_moe(x, router_w, router_b, w_in, w_down):
    b, s, d = x.shape
    xt = x.reshape(-1, d)
    t = xt.shape[0]
    aff = jax.nn.sigmoid((xt @ router_w).astype(jnp.float32))
    biased = aff + router_b.astype(jnp.float32)
    group_score = lax.top_k(biased.reshape(t, N_GROUPS, EXPERTS_PER_GROUP), 2)[0].sum(-1)
    top_group = jnp.argmax(group_score, axis=-1)
    group_mask = jnp.arange(N_GROUPS)[None, :] == top_group[:, None]
    expert_mask = jnp.repeat(group_mask, EXPERTS_PER_GROUP, axis=1)
    _, idx = lax.top_k(jnp.where(expert_mask, biased, -jnp.inf), TOP_K)
    w = jnp.take_along_axis(aff, idx, axis=-1)
    w = w / w.sum(-1, keepdims=True)
    gates = (jax.nn.one_hot(idx, N_EXPERTS, dtype=jnp.float32) * w[..., None]).sum(1)
    y = jnp.zeros((t, d), jnp.float32)
    for e in range(N_EXPERTS):
        h_gate, h_up = jnp.split(xt @ w_in[e], 2, axis=-1)
        y = y + gates[:, e:e + 1] * ((jax.nn.silu(h_gate) * h_up) @ w_down[e])
    return y.astype(x.dtype).reshape(b, s, d)


def per_layer_input(x, p_i, w_gate, b_gate, w_proj):
    gate = jax.nn.sigmoid(x @ w_gate + b_gate)
    return gate * (p_i @ w_proj)


def setup_inputs(seed: int = 0) -> dict:
    key = jax.random.key(seed)
    ks = jax.random.split(key, 40)
    f32 = jnp.float32

    def nrm(k, shape, fan_in, scale=1.0):
        return jax.random.normal(k, shape, f32) * (scale * fan_in ** -0.5)

    def gain(k, shape):
        return 1.0 + 0.02 * jax.random.normal(k, shape, f32)

    def bias(k, shape):
        return 0.02 * jax.random.normal(k, shape, f32)

    d = D_MODEL
    positions = (jnp.arange(SEQ, dtype=jnp.int32)[None, :]
                 + jax.random.randint(ks[2], (BATCH, 1), 0, MAX_POS_OFFSET, dtype=jnp.int32))
    kv_w_up = jnp.concatenate([
        nrm(ks[13], (KV_LORA_RANK, N_HEADS, QK_NOPE_DIM), KV_LORA_RANK),
        nrm(ks[14], (KV_LORA_RANK, N_HEADS, V_HEAD_DIM), KV_LORA_RANK, BETA)], axis=-1)
    return {
        "x": jax.random.normal(ks[0], (BATCH, SEQ, d), f32),
        "p": jax.random.normal(ks[1], (DEPTH, BATCH, SEQ, PLE_DIM), f32),
        "positions": positions,
        "conv_w_in": nrm(ks[3], (N_A_LAYERS, d, 2 * d), d),
        "conv_b_in": bias(ks[4], (N_A_LAYERS, 2 * d)),
        "conv_w_dw": nrm(ks[5], (N_A_LAYERS, CONV_KERNEL, 1, d), CONV_KERNEL),
        "conv_b_dw": bias(ks[6], (N_A_LAYERS, d)),
        "conv_ln_g": gain(ks[7], (N_A_LAYERS, d)),
        "conv_ln_b": bias(ks[8], (N_A_LAYERS, d)),
        "conv_w_out": nrm(ks[9], (N_A_LAYERS, d, d), d, BETA),
        "conv_b_out": bias(ks[10], (N_A_LAYERS, d)),
        "kv_w_down": nrm(ks[11], (d, KV_LORA_RANK + QK_ROPE_DIM), d),
        "kv_norm_g": gain(ks[12], (KV_LORA_RANK,)),
        "kv_w_up": kv_w_up,
        "q_w_down": nrm(ks[15], (N_B_LAYERS, d, Q_LORA_RANK), d),
        "q_norm_g": gain(ks[16], (N_B_LAYERS, Q_LORA_RANK)),
        "q_w_up": nrm(ks[17], (N_B_LAYERS, Q_LORA_RANK, N_HEADS, QK_NOPE_DIM + QK_ROPE_DIM), Q_LORA_RANK),
        "attn_w_out": nrm(ks[18], (N_B_LAYERS, N_HEADS * V_HEAD_DIM, d), N_HEADS * V_HEAD_DIM, BETA),
        "router_w": nrm(ks[19], (d, N_EXPERTS), d),
        "router_b": 0.01 * jax.random.normal(ks[20], (N_EXPERTS,), f32),
        "moe_w_in": nrm(ks[21], (DEPTH, N_EXPERTS, d, 2 * D_EXPERT), d),
        "moe_w_down": nrm(ks[22], (DEPTH, N_EXPERTS, D_EXPERT, d), D_EXPERT, BETA),
        "ln1_g": gain(ks[23], (DEPTH, d)),
        "ln1_b": bias(ks[24], (DEPTH, d)),
        "ln2_g": gain(ks[25], (DEPTH, d)),
        "ln2_b": bias(ks[26], (DEPTH, d)),
        "ple_w_gate": nrm(ks[27], (DEPTH, d, d), d),
        "ple_b_gate": bias(ks[28], (DEPTH, d)),
        "ple_w_proj": nrm(ks[29], (DEPTH, PLE_DIM, d), PLE_DIM, BETA),
    }


def reference(x, p, positions, conv_w_in, conv_b_in, conv_w_dw, conv_b_dw, conv_ln_g, conv_ln_b,
              conv_w_out, conv_b_out, kv_w_down, kv_norm_g, kv_w_up, q_w_down, q_norm_g, q_w_up,
              attn_w_out, router_w, router_b, moe_w_in, moe_w_down, ln1_g, ln1_b, ln2_g, ln2_b,
              ple_w_gate, ple_b_gate, ple_w_proj):
    cos, sin = rope_tables(positions, x.dtype)
    kv = None
    for i in range(DEPTH):
        if i < N_A_LAYERS:
            mix = conformer_conv(x, conv_w_in[i], conv_b_in[i], conv_w_dw[i], conv_b_dw[i],
                                 conv_ln_g[i], conv_ln_b[i], conv_w_out[i], conv_b_out[i])
        else:
            j = i - N_A_LAYERS
            k_nope, k_pe, v = kv
            mix = mla_layer(x, k_nope, k_pe, v, cos, sin, q_w_down[j], q_norm_g[j], q_w_up[j], attn_w_out[j])
        x = layer_norm(ALPHA * x + mix, ln1_g[i], ln1_b[i])
        x = layer_norm(ALPHA * x + grouped_moe(x, router_w, router_b, moe_w_in[i], moe_w_down[i]),
                       ln2_g[i], ln2_b[i])
        x = x + per_layer_input(x, p[i], ple_w_gate[i], ple_b_gate[i], ple_w_proj[i])
        if i == N_A_LAYERS - 1 and N_B_LAYERS > 0:
            kv = shared_kv(x, kv_w_down, kv_norm_g, kv_w_up, cos, sin)
    return x
```

```python
import functools

import jax
import jax.numpy as jnp
from jax import lax
from jax.experimental import pallas as pl
from jax.experimental.pallas import tpu as pltpu

F32 = jnp.float32
BF16 = jnp.bfloat16

CONV_KERNEL = 31
N_HEADS = 32
QK_NOPE_DIM = 128
QK_ROPE_DIM = 64
V_HEAD_DIM = 128
KV_LORA_RANK = 512
ROPE_THETA = 10000.0
SOFTMAX_SCALE = (QK_NOPE_DIM + QK_ROPE_DIM) ** -0.5
N_EXPERTS = 32
N_GROUPS = 8
EXPERTS_PER_GROUP = N_EXPERTS // N_GROUPS
DEPTH = 2
ALPHA = (2 * DEPTH) ** 0.25
LN_EPS = 1e-5
RMS_EPS = 1e-6

LANES = 128
MOE_ROW_TILE = 512
VMEM_LIMIT = 56 * 1024 * 1024


def _cparams(*sem, vmem=VMEM_LIMIT):
    return pltpu.CompilerParams(dimension_semantics=sem, vmem_limit_bytes=vmem)


def _pick(n, pref):
    t = min(n, pref)
    while n % t:
        t //= 2
    return t


def _sigmoid(x):
    return 1.0 / (1.0 + jnp.exp(-x))


def _layer_norm(r, g, b):
    mu = jnp.mean(r, axis=-1, keepdims=True)
    rc = r - mu
    var = jnp.mean(rc * rc, axis=-1, keepdims=True)
    return rc * lax.rsqrt(var + LN_EPS) * g + b


def _mm_kernel(a_ref, w_ref, b_ref, o_ref, acc_ref, *, nk, scale):
    k = pl.program_id(2)

    @pl.when(k == 0)
    def _():
        acc_ref[...] = jnp.zeros_like(acc_ref)

    acc_ref[...] += jnp.dot(a_ref[...], w_ref[...], preferred_element_type=F32)

    @pl.when(k == nk - 1)
    def _():
        o_ref[...] = ((acc_ref[...] + b_ref[...]) * scale).astype(o_ref.dtype)


def _matmul(a, w, bias=None, *, out_dtype=BF16, scale=1.0, tm=1024, tn=1024, tk=512):
    m, kd = a.shape
    n = w.shape[1]
    tm, tn, tk = _pick(m, tm), _pick(n, tn), _pick(kd, tk)
    if bias is None:
        bias = jnp.zeros((1, n), F32)
    nk = kd // tk
    return pl.pallas_call(
        functools.partial(_mm_kernel, nk=nk, scale=scale),
        grid=(m // tm, n // tn, nk),
        in_specs=[
            pl.BlockSpec((tm, tk), lambda i, j, k: (i, k)),
            pl.BlockSpec((tk, tn), lambda i, j, k: (k, j)),
            pl.BlockSpec((1, tn), lambda i, j, k: (0, j)),
        ],
        out_specs=pl.BlockSpec((tm, tn), lambda i, j, k: (i, j)),
        out_shape=jax.ShapeDtypeStruct((m, n), out_dtype),
        scratch_shapes=[pltpu.VMEM((tm, tn), F32)],
        compiler_params=_cparams("parallel", "parallel", "arbitrary"),
        name="matmul",
    )(a, w, bias)


def _glu_kernel(a_ref, wa_ref, wg_ref, ba_ref, bg_ref, o_ref, acca_ref, accg_ref, *, nk):
    k = pl.program_id(2)

    @pl.when(k == 0)
    def _():
        acca_ref[...] = jnp.zeros_like(acca_ref)
        accg_ref[...] = jnp.zeros_like(accg_ref)

    a = a_ref[...]
    acca_ref[...] += jnp.dot(a, wa_ref[...], preferred_element_type=F32)
    accg_ref[...] += jnp.dot(a, wg_ref[...], preferred_element_type=F32)

    @pl.when(k == nk - 1)
    def _():
        val = acca_ref[...] + ba_ref[...]
        gate = accg_ref[...] + bg_ref[...]
        o_ref[...] = (val * _sigmoid(gate)).astype(o_ref.dtype)


def _glu_matmul(a, w, bias, *, tm=1024, tn=512, tk=512):
    m, kd = a.shape
    n = w.shape[1] // 2
    tm, tn, tk = _pick(m, tm), _pick(n, tn), _pick(kd, tk)
    nk, nj = kd // tk, n // tn
    return pl.pallas_call(
        functools.partial(_glu_kernel, nk=nk),
        grid=(m // tm, nj, nk),
        in_specs=[
            pl.BlockSpec((tm, tk), lambda i, j, k: (i, k)),
            pl.BlockSpec((tk, tn), lambda i, j, k: (k, j)),
            pl.BlockSpec((tk, tn), lambda i, j, k: (k, j + nj)),
            pl.BlockSpec((1, tn), lambda i, j, k: (0, j)),
            pl.BlockSpec((1, tn), lambda i, j, k: (0, j + nj)),
        ],
        out_specs=pl.BlockSpec((tm, tn), lambda i, j, k: (i, j)),
        out_shape=jax.ShapeDtypeStruct((m, n), F32),
        scratch_shapes=[pltpu.VMEM((tm, tn), F32), pltpu.VMEM((tm, tn), F32)],
        compiler_params=_cparams("parallel", "parallel", "arbitrary"),
        name="glu_matmul",
    )(a, w, w, bias, bias)


SUBLANES = 8
CONV_HALO = 32
CONV_ROWS = 64
CONV_LEAD = CONV_HALO - (CONV_KERNEL - 1)


def _conv_ln_kernel(prev_ref, cur_ref, w_ref, bdw_ref, g_ref, b_ref, o_ref, win_ref, conv_ref, *, ts, d):
    i = pl.program_id(1)

    @pl.when(i == 0)
    def _():
        win_ref[0:CONV_HALO, :] = jnp.zeros((CONV_HALO, d), F32)

    @pl.when(i > 0)
    def _():
        win_ref[0:CONV_HALO, :] = prev_ref[...]

    win_ref[CONV_HALO:CONV_HALO + ts, :] = cur_ref[...]
    win_ref[CONV_HALO + ts:, :] = jnp.zeros((SUBLANES, d), F32)

    def cols_body(cb, carry):
        cols = pl.ds(pl.multiple_of(cb * LANES, LANES), LANES)
        for r0 in range(0, ts, CONV_ROWS):
            acc = None
            for c in range(SUBLANES):
                u = None
                for a in range((CONV_LEAD + CONV_KERNEL - 1) // SUBLANES + 1):
                    k = SUBLANES * a + c - CONV_LEAD
                    if 0 <= k < CONV_KERNEL:
                        term = win_ref[pl.ds(r0 + SUBLANES * a, CONV_ROWS + SUBLANES), cols] * w_ref[k:k + 1, cols]
                        u = term if u is None else u + term
                part = u[c:c + CONV_ROWS]
                acc = part if acc is None else acc + part
            conv_ref[pl.ds(r0, CONV_ROWS), cols] = acc
        return carry

    lax.fori_loop(0, d // LANES, cols_body, 0)

    h = _layer_norm(conv_ref[...] + bdw_ref[...], g_ref[...], b_ref[...])
    o_ref[...] = (h * _sigmoid(h)).astype(o_ref.dtype)


def _conv_ln_swish(h, w_dw, b_dw, g, b, *, batch, seq, ts=256):
    t, d = h.shape
    ts = _pick(seq, ts)
    nblk = seq // ts
    hb = ts // CONV_HALO
    return pl.pallas_call(
        functools.partial(_conv_ln_kernel, ts=ts, d=d),
        grid=(batch, nblk),
        in_specs=[
            pl.BlockSpec((CONV_HALO, d), lambda bi, i: (jnp.maximum((bi * nblk + i) * hb - 1, 0), 0)),
            pl.BlockSpec((ts, d), lambda bi, i: (bi * nblk + i, 0)),
            pl.BlockSpec((CONV_KERNEL, d), lambda bi, i: (0, 0)),
            pl.BlockSpec((1, d), lambda bi, i: (0, 0)),
            pl.BlockSpec((1, d), lambda bi, i: (0, 0)),
            pl.BlockSpec((1, d), lambda bi, i: (0, 0)),
        ],
        out_specs=pl.BlockSpec((ts, d), lambda bi, i: (bi * nblk + i, 0)),
        out_shape=jax.ShapeDtypeStruct((t, d), BF16),
        scratch_shapes=[pltpu.VMEM((CONV_HALO + ts + SUBLANES, d), F32), pltpu.VMEM((ts, d), F32)],
        compiler_params=_cparams("parallel", "arbitrary"),
        name="conv_ln_swish",
    )(h, h, w_dw, b_dw, g, b)


def _mm_ln_kernel(a_ref, w_ref, bias_ref, res_ref, g_ref, b_ref, rwh_ref, rwl_ref,
                  x_ref, lg_ref, acc_ref, *, nk):
    k = pl.program_id(1)

    @pl.when(k == 0)
    def _():
        acc_ref[...] = jnp.zeros_like(acc_ref)

    acc_ref[...] += jnp.dot(a_ref[...], w_ref[...], preferred_element_type=F32)

    @pl.when(k == nk - 1)
    def _():
        r = ALPHA * res_ref[...] + (acc_ref[...] + bias_ref[...])
        y = _layer_norm(r, g_ref[...], b_ref[...])
        x_ref[...] = y
        yh = y.astype(BF16)
        yl =(y - yh.astype(F32)).astype(BF16)
        nt = (((1,), (1,)), ((), ()))
        lg = lax.dot_general(rwh_ref[...], yh, nt, preferred_element_type=F32)
        lg = lg + lax.dot_general(rwh_ref[...], yl, nt, preferred_element_type=F32)
        lg = lg + lax.dot_general(rwl_ref[...], yh, nt, preferred_element_type=F32)
        lg_ref[...] = lg


def _matmul_ln_router(a, w, bias, res, g, b, rw_hi, rw_lo, *, tm=256, tk=512):
    m, kd = a.shape
    d = w.shape[1]
    tm, tk = _pick(m, tm), _pick(kd, tk)
    nk = kd // tk
    ne = rw_hi.shape[0]
    row = lambda i, k: (i, 0)
    const = lambda i, k: (0, 0)
    return pl.pallas_call(
        functools.partial(_mm_ln_kernel, nk=nk),
        grid=(m // tm, nk),
        in_specs=[
            pl.BlockSpec((tm, tk), lambda i, k: (i, k)),
            pl.BlockSpec((tk, d), lambda i, k: (k, 0)),
            pl.BlockSpec((1, d), const),
            pl.BlockSpec((tm, d), row),
            pl.BlockSpec((1, d), const),
            pl.BlockSpec((1, d), const),
            pl.BlockSpec((ne, d), const),
            pl.BlockSpec((ne, d), const),
        ],
        out_specs=[
            pl.BlockSpec((tm, d), row),
            pl.BlockSpec((ne, tm), lambda i, k: (0, i)),
        ],
        out_shape=[
            jax.ShapeDtypeStruct((m, d), F32),
            jax.ShapeDtypeStruct((ne, m), F32),
        ],
        scratch_shapes=[pltpu.VMEM((tm, d), F32)],
        compiler_params=_cparams("parallel", "arbitrary"),
        name="matmul_ln_router",
    )(a, w, bias, res, g, b, rw_hi, rw_lo)


def _route_kernel(lg_ref, rb_ref, e_ref, w_ref, rank_ref, cnt_ref, carry_ref, *, tt, nsteps):
    step = pl.program_id(0)

    @pl.when(step == 0)
    def _():
        carry_ref[...] = jnp.zeros_like(carry_ref)

    aff = _sigmoid(lg_ref[...])
    biased = aff + rb_ref[...]
    rows_b = [biased[e:e + 1, :] for e in range(N_EXPERTS)]
    rows_a = [aff[e:e + 1, :] for e in range(N_EXPERTS)]

    def top2_sum(v):
        m01, n01 = jnp.maximum(v[0], v[1]), jnp.minimum(v[0], v[1])
        m23, n23 = jnp.maximum(v[2], v[3]), jnp.minimum(v[2], v[3])
        return jnp.maximum(m01, m23) + jnp.maximum(jnp.minimum(m01, m23), jnp.maximum(n01, n23))

    best = top2_sum(rows_b[0:EXPERTS_PER_GROUP])
    gidx = jnp.zeros((1, tt), jnp.int32)
    for g in range(1, N_GROUPS):
        s = top2_sum(rows_b[g * EXPERTS_PER_GROUP:(g + 1) * EXPERTS_PER_GROUP])
        better = s > best
        gidx = jnp.where(better, g, gidx)
        best = jnp.where(better, s, best)

    vb = [rows_b[i] for i in range(EXPERTS_PER_GROUP)]
    va = [rows_a[i] for i in range(EXPERTS_PER_GROUP)]
    for g in range(1, N_GROUPS):
        sel = gidx == g
        for i in range(EXPERTS_PER_GROUP):
            vb[i] = jnp.where(sel, rows_b[g * EXPERTS_PER_GROUP + i], vb[i])
            va[i] = jnp.where(sel, rows_a[g * EXPERTS_PER_GROUP + i], va[i])

    i1 = jnp.zeros((1, tt), jnp.int32)
    b1, a1 = vb[0], va[0]
    for i in range(1, EXPERTS_PER_GROUP):
        better = vb[i] > b1
        i1 = jnp.where(better, i, i1)
        b1 = jnp.where(better, vb[i], b1)
        a1 = jnp.where(better, va[i], a1)
    i2 = jnp.where(i1 == 0, 1, 0).astype(jnp.int32)
    b2 = jnp.where(i1 == 0, vb[1], vb[0])
    a2 = jnp.where(i1 == 0, va[1], va[0])
    for i in range(1, EXPERTS_PER_GROUP):
        better = jnp.logical_and(vb[i] > b2, i1 != i)
        i2 = jnp.where(better, i, i2)
        b2 = jnp.where(better, vb[i], b2)
        a2 = jnp.where(better, va[i], a2)

    e1 = gidx * EXPERTS_PER_GROUP + i1
    e2 = gidx * EXPERTS_PER_GROUP + i2
    den = a1 + a2
    e_ref[0:1, :] = e1
    e_ref[1:2, :] = e2
    w_ref[0:1, :] = a1 / den
    w_ref[1:2, :] = a2 / den

    eid = lax.broadcasted_iota(jnp.int32, (N_EXPERTS, tt), 0)
    hit1 = eid == e1
    hit2 = eid == e2
    onehot = jnp.where(jnp.logical_or(hit1, hit2), 1.0, 0.0)
    src = lax.broadcasted_iota(jnp.int32, (tt, tt), 0)
    dst = lax.broadcasted_iota(jnp.int32, (tt, tt), 1)
    upper = jnp.where(src < dst, 1.0, 0.0).astype(BF16)
    before = jnp.dot(onehot.astype(BF16), upper, preferred_element_type=F32) + carry_ref[:, 0:1]
    rank_ref[0:1, :] = jnp.sum(jnp.where(hit1, before, 0.0), axis=0, keepdims=True).astype(jnp.int32)
    rank_ref[1:2, :] = jnp.sum(jnp.where(hit2, before, 0.0), axis=0, keepdims=True).astype(jnp.int32)
    carry_ref[...] += jnp.sum(onehot, axis=1, keepdims=True)

    @pl.when(step == nsteps - 1)
    def _():
        cnt_ref[...] = carry_ref[...]


def _route(logits_t, router_b, *, tt=512):
    ne, t = logits_t.shape
    tt = _pick(t, tt)
    nsteps = t // tt
    return pl.pallas_call(
        functools.partial(_route_kernel, tt=tt, nsteps=nsteps),
        grid=(nsteps,),
        in_specs=[
            pl.BlockSpec((ne, tt), lambda s: (0, s)),
            pl.BlockSpec((ne, 1), lambda s: (0, 0)),
        ],
        out_specs=[
            pl.BlockSpec((2, tt), lambda s: (0, s)),
            pl.BlockSpec((2, tt), lambda s: (0, s)),
            pl.BlockSpec((2, tt), lambda s: (0, s)),
            pl.BlockSpec((ne, LANES), lambda s: (0, 0)),
        ],
        out_shape=[
            jax.ShapeDtypeStruct((2, t), jnp.int32),
            jax.ShapeDtypeStruct((2, t), F32),
            jax.ShapeDtypeStruct((2, t), jnp.int32),
            jax.ShapeDtypeStruct((ne, LANES), F32),
        ],
        scratch_shapes=[pltpu.VMEM((ne, LANES), F32)],
        compiler_params=_cparams("arbitrary"),
        name="route",
    )(logits_t, router_b.reshape(ne, 1).astype(F32))


def _row_copy(src_ref, dst_ref, sem, src_row, dst_row):
    return pltpu.make_async_copy(src_ref.at[pl.ds(src_row, 1)], dst_ref.at[pl.ds(dst_row, 1)], sem)


def _gather_rows_kernel(idx_ref, src_ref, o_ref, buf_ref, sem, *, rt):
    def issue(r, carry):
        _row_copy(src_ref, buf_ref, sem, idx_ref[0, 0, r], r).start()
        return carry

    lax.fori_loop(0, rt, issue, 0)

    def drain(r, carry):
        _row_copy(src_ref, buf_ref, sem, 0, r).wait()
        return carry

    lax.fori_loop(0, rt, drain, 0)
    o_ref[...] = buf_ref[...].astype(o_ref.dtype)


def _gather_rows(src, idx, *, out_dtype, rt=256):
    n = idx.shape[0]
    d = src.shape[1]
    rt = _pick(n, rt)
    return pl.pallas_call(
        functools.partial(_gather_rows_kernel, rt=rt),
        grid=(n // rt,),
        in_specs=[
            pl.BlockSpec((1, 1, rt), lambda i: (i, 0, 0), memory_space=pltpu.SMEM),
            pl.BlockSpec(memory_space=pl.ANY),
        ],
        out_specs=pl.BlockSpec((rt, d), lambda i: (i, 0)),
        out_shape=jax.ShapeDtypeStruct((n, d), out_dtype),
        scratch_shapes=[pltpu.VMEM((rt, d), src.dtype), pltpu.SemaphoreType.DMA],
        compiler_params=_cparams("arbitrary"),
        name="gather_rows",
    )(idx.reshape(n // rt, 1, rt), src)


def _ffn_up_kernel(te_ref, tv_ref, x_ref, wg_ref, wu_ref, o_ref, accg_ref, accu_ref, *, nk):
    i = pl.program_id(0)
    k = pl.program_id(1)
    valid = tv_ref[i] > 0

    @pl.when(jnp.logical_and(valid, k == 0))
    def _():
        accg_ref[...] = jnp.zeros_like(accg_ref)
        accu_ref[...] = jnp.zeros_like(accu_ref)

    @pl.when(valid)
    def _():
        x = x_ref[...]
        accg_ref[...] += jnp.dot(x, wg_ref[...].astype(BF16), preferred_element_type=F32)
        accu_ref[...] += jnp.dot(x, wu_ref[...].astype(BF16), preferred_element_type=F32)

    @pl.when(jnp.logical_and(valid, k == nk - 1))
    def _():
        hg = accg_ref[...]
        o_ref[...] = (hg * _sigmoid(hg) * accu_ref[...]).astype(o_ref.dtype)

    @pl.when(jnp.logical_and(jnp.logical_not(valid), k == nk - 1))
    def _():
        o_ref[...] = jnp.zeros_like(o_ref)


def _ffn_up(xs, w_in, tile_expert, tile_valid, *, tm, tk=512):
    r, d = xs.shape
    de = w_in.shape[2] // 2
    tk = _pick(d, tk)
    nk = d // tk
    n_tiles = r // tm

    def kk(i, k, tv):
        return k * tv[i] + (nk - 1) * (1 - tv[i])

    def ii(i, tv):
        return jnp.maximum(jnp.minimum(i, tv[n_tiles] - 1), 0)

    grid_spec = pltpu.PrefetchScalarGridSpec(
        num_scalar_prefetch=2,
        grid=(n_tiles, nk),
        in_specs=[
            pl.BlockSpec((tm, tk), lambda i, k, te, tv: (ii(i, tv), kk(i, k, tv))),
            pl.BlockSpec((None, tk, de), lambda i, k, te, tv: (te[i], kk(i, k, tv), 0)),
            pl.BlockSpec((None, tk, de), lambda i, k, te, tv: (te[i], kk(i, k, tv), 1)),
        ],
        out_specs=pl.BlockSpec((tm, de), lambda i, k, te, tv: (i, 0)),
        scratch_shapes=[pltpu.VMEM((tm, de), F32), pltpu.VMEM((tm, de), F32)],
    )
    return pl.pallas_call(
        functools.partial(_ffn_up_kernel, nk=nk),
        grid_spec=grid_spec,
        out_shape=jax.ShapeDtypeStruct((r, de), BF16),
        compiler_params=_cparams("arbitrary", "arbitrary"),
        name="ffn_up",
    )(tile_expert, tile_valid, xs, w_in, w_in)


def _ffn_down_kernel(te_ref, tv_ref, h_ref, w_ref, o_ref):
    i = pl.program_id(0)
    valid = tv_ref[i] > 0

    @pl.when(valid)
    def _():
        o_ref[...] = jnp.dot(h_ref[...], w_ref[...].astype(BF16), preferred_element_type=F32).astype(o_ref.dtype)

    @pl.when(jnp.logical_not(valid))
    def _():
        o_ref[...] = jnp.zeros_like(o_ref)


def _ffn_down(h, w_down, tile_expert, tile_valid, *, tm, tn=1024):
    r, de = h.shape
    d = w_down.shape[2]
    tn = _pick(d, tn)
    nj = d // tn
    n_tiles = r // tm

    def jj(i, j, tv):
        return j * tv[i] + (nj - 1) * (1 - tv[i])

    def ii(i, tv):
        return jnp.maximum(jnp.minimum(i, tv[n_tiles] - 1), 0)

    grid_spec = pltpu.PrefetchScalarGridSpec(
        num_scalar_prefetch=2,
        grid=(n_tiles, nj),
        in_specs=[
            pl.BlockSpec((tm, de), lambda i, j, te, tv: (ii(i, tv), 0)),
            pl.BlockSpec((None, de, tn), lambda i, j, te, tv: (te[i], 0, jj(i, j, tv))),
        ],
        out_specs=pl.BlockSpec((tm, tn), lambda i, j, te, tv: (i, j)),
    )
    return pl.pallas_call(
        _ffn_down_kernel,
        grid_spec=grid_spec,
        out_shape=jax.ShapeDtypeStruct((r, d), F32),
        compiler_params=_cparams("arbitrary", "arbitrary"),
        name="ffn_down",
    )(tile_expert, tile_valid, h, w_down)


def _combine_ln_kernel(pos_ref, ys_ref, gw_ref, x_ref, g_ref, b_ref, xo_ref, xb_ref, buf_ref, sem, *, tm):
    def issue(r, carry):
        _row_copy(ys_ref, buf_ref.at[0], sem, pos_ref[0, 0, r], r).start()
        _row_copy(ys_ref, buf_ref.at[1], sem, pos_ref[0, 0, tm + r], r).start()
        return carry

    lax.fori_loop(0, tm, issue, 0)

    def drain(r, carry):
        _row_copy(ys_ref, buf_ref.at[0], sem, 0, r).wait()
        _row_copy(ys_ref, buf_ref.at[1], sem, 0, r).wait()
        return carry

    lax.fori_loop(0, tm, drain, 0)
    gw = gw_ref[...]
    y = gw[:, 0:1] * buf_ref[0] + gw[:, 1:2] * buf_ref[1]
    out = _layer_norm(ALPHA * x_ref[...] + y, g_ref[...], b_ref[...])
    xo_ref[...] = out
    xb_ref[...] = out.astype(BF16)


def _combine_ln(ys, pos, gate_w, x, g, b, *, tm=256):
    t, d = x.shape
    tm = _pick(t, tm)
    nb = t // tm
    pos_tiles = pos.reshape(2, nb, tm).transpose(1, 0, 2).reshape(nb, 1, 2 * tm)
    row = lambda i: (i, 0)
    const = lambda i: (0, 0)
    return pl.pallas_call(
        functools.partial(_combine_ln_kernel, tm=tm),
        grid=(nb,),
        in_specs=[
            pl.BlockSpec((1, 1, 2 * tm), lambda i: (i, 0, 0), memory_space=pltpu.SMEM),
            pl.BlockSpec(memory_space=pl.ANY),
            pl.BlockSpec((tm, 2), row),
            pl.BlockSpec((tm, d), row),
            pl.BlockSpec((1, d), const),
            pl.BlockSpec((1, d), const),
        ],
        out_specs=[pl.BlockSpec((tm, d), row), pl.BlockSpec((tm, d), row)],
        out_shape=[jax.ShapeDtypeStruct((t, d), F32), jax.ShapeDtypeStruct((t, d), BF16)],
        scratch_shapes=[pltpu.VMEM((2, tm, d), ys.dtype), pltpu.SemaphoreType.DMA],
        compiler_params=_cparams("arbitrary"),
        name="combine_ln",
    )(pos_tiles, ys, gate_w, x, g, b)


def _ple_kernel(a_ref, wg_ref, bg_ref, p_ref, wp_ref, x_ref, xo_ref, xb_ref, acc_ref, *, nk):
    k = pl.program_id(2)

    @pl.when(k == 0)
    def _():
        acc_ref[...] = jnp.zeros_like(acc_ref)

    acc_ref[...] += jnp.dot(a_ref[...], wg_ref[...], preferred_element_type=F32)

    @pl.when(k == nk - 1)
    def _():
        gate = _sigmoid(acc_ref[...] + bg_ref[...])
        proj = jnp.dot(p_ref[...], wp_ref[...], preferred_element_type=F32)
        out = x_ref[...] + gate * proj
        xo_ref[...] = out
        xb_ref[...] = out.astype(BF16)


def _per_layer_input(xb, x, p, w_gate, b_gate, w_proj, *, tm=1024, tn=1024, tk=512):
    m, d = x.shape
    pd = p.shape[1]
    tm, tn, tk = _pick(m, tm), _pick(d, tn), _pick(d, tk)
    nk = d // tk
    tile = lambda i, j, k: (i, j)
    return pl.pallas_call(
        functools.partial(_ple_kernel, nk=nk),
        grid=(m // tm, d // tn, nk),
        in_specs=[
            pl.BlockSpec((tm, tk), lambda i, j, k: (i, k)),
            pl.BlockSpec((tk, tn), lambda i, j, k: (k, j)),
            pl.BlockSpec((1, tn), lambda i, j, k: (0, j)),
            pl.BlockSpec((tm, pd), lambda i, j, k: (i, 0)),
            pl.BlockSpec((pd, tn), lambda i, j, k: (0, j)),
            pl.BlockSpec((tm, tn), tile),
        ],
        out_specs=[pl.BlockSpec((tm, tn), tile), pl.BlockSpec((tm, tn), tile)],
        out_shape=[jax.ShapeDtypeStruct((m, d), F32), jax.ShapeDtypeStruct((m, d), BF16)],
        scratch_shapes=[pltpu.VMEM((tm, tn), F32)],
        compiler_params=_cparams("parallel", "parallel", "arbitrary"),
        name="per_layer_input",
    )(xb, w_gate, b_gate, p, w_proj, x)


def _rope_table_kernel(pos_ref, cos_ref, sin_ref):
    half = QK_ROPE_DIM // 2
    lane = lax.broadcasted_iota(jnp.int32, (1, LANES), 1)
    freq = (lane % half).astype(F32)
    inv_freq = 1.0 / (ROPE_THETA ** (freq * (2.0 / QK_ROPE_DIM)))
    ang = pos_ref[...].astype(F32) * inv_freq
    sign = jnp.where((lane // half) % 2 == 0, -1.0, 1.0)
    cos_ref[...] = jnp.cos(ang)
    sin_ref[...] = jnp.sin(ang) * sign


def _rope_tables(positions, *, tm=2048):
    t = positions.size
    tm = _pick(t, tm)
    return pl.pallas_call(
        _rope_table_kernel,
        grid=(t // tm,),
        in_specs=[pl.BlockSpec((tm, 1), lambda i: (i, 0))],
        out_specs=[pl.BlockSpec((tm, LANES), lambda i: (i, 0))] * 2,
        out_shape=[jax.ShapeDtypeStruct((t, LANES), F32)] * 2,
        compiler_params=_cparams("parallel"),
        name="rope_tables",
    )(positions.reshape(t, 1))


def _latent_kernel(a_ref, w_ref, gq_ref, gkv_ref, cos_ref, sin_ref, cq_ref, ckv_ref, kr_ref, acc_ref, *, nk, qr):
    k = pl.program_id(1)

    @pl.when(k == 0)
    def _():
        acc_ref[...] = jnp.zeros_like(acc_ref)

    acc_ref[...] += jnp.dot(a_ref[...], w_ref[...], preferred_element_type=F32)

    @pl.when(k == nk - 1)
    def _():
        c = acc_ref[...]
        cq = c[:, :qr]
        cq_ref[...] = (cq * lax.rsqrt(jnp.mean(cq * cq, axis=-1, keepdims=True) + RMS_EPS)
                       * gq_ref[...]).astype(BF16)
        ckv = c[:, qr:qr + KV_LORA_RANK]
        ckv_ref[...] = (ckv * lax.rsqrt(jnp.mean(ckv * ckv, axis=-1, keepdims=True) + RMS_EPS)
                        * gkv_ref[...]).astype(BF16)
        kpe = c[:, qr + KV_LORA_RANK:]
        cos = cos_ref[...]
        sin = sin_ref[...]
        rot = kpe[:, :QK_ROPE_DIM] * cos[:, :QK_ROPE_DIM] + kpe[:, QK_ROPE_DIM:] * sin[:, :QK_ROPE_DIM]
        kr_ref[...] = rot.astype(BF16)


def _latents(xb, w_cat, gq, gkv, cos_t, sin_t, *, qr, tm=512, tk=512):
    m, kd = xb.shape
    n = w_cat.shape[1]
    tm, tk = _pick(m, tm), _pick(kd, tk)
    nk = kd // tk
    row = lambda i, k: (i, 0)
    const = lambda i, k: (0, 0)
    return pl.pallas_call(
        functools.partial(_latent_kernel, nk=nk, qr=qr),
        grid=(m // tm, nk),
        in_specs=[
            pl.BlockSpec((tm, tk), lambda i, k: (i, k)),
            pl.BlockSpec((tk, n), lambda i, k: (k, 0)),
            pl.BlockSpec((1, qr), const),
            pl.BlockSpec((1, KV_LORA_RANK), const),
            pl.BlockSpec((tm, LANES), row),
            pl.BlockSpec((tm, LANES), row),
        ],
        out_specs=[
            pl.BlockSpec((tm, qr), row),
            pl.BlockSpec((tm, KV_LORA_RANK), row),
            pl.BlockSpec((tm, QK_ROPE_DIM), row),
        ],
        out_shape=[
            jax.ShapeDtypeStruct((m, qr), BF16),
            jax.ShapeDtypeStruct((m, KV_LORA_RANK), BF16),
            jax.ShapeDtypeStruct((m, QK_ROPE_DIM), BF16),
        ],
        scratch_shapes=[pltpu.VMEM((tm, n), F32)],
        compiler_params=_cparams("parallel", "arbitrary"),
        name="latents",
    )(xb, w_cat, gq, gkv, cos_t, sin_t)


def _rope_q_kernel(a_ref, w_ref, ws_ref, cos_ref, sin_ref, o_ref, *, scale, reps):
    a = a_ref[...]
    q = jnp.dot(a, w_ref[...], preferred_element_type=F32)
    qs = jnp.dot(a, ws_ref[...], preferred_element_type=F32)
    cos = jnp.tile(cos_ref[...], (1, reps))
    sin = jnp.tile(sin_ref[...], (1, reps))
    o_ref[...] = ((q * cos + qs * sin) * scale).astype(o_ref.dtype)


def _rope_q(cq, w_r, w_r_swapped, cos_t, sin_t, *, scale, tm=1024, tn=1024):
    m, kd = cq.shape
    n = w_r.shape[1]
    tm, tn = _pick(m, tm), _pick(n, tn)
    return pl.pallas_call(
        functools.partial(_rope_q_kernel, scale=scale, reps=tn // LANES),
        grid=(m // tm, n // tn),
        in_specs=[
            pl.BlockSpec((tm, kd), lambda i, j: (i, 0)),
            pl.BlockSpec((kd, tn), lambda i, j: (0, j)),
            pl.BlockSpec((kd, tn), lambda i, j: (0, j)),
            pl.BlockSpec((tm, LANES), lambda i, j: (i, 0)),
            pl.BlockSpec((tm, LANES), lambda i, j: (i, 0)),
        ],
        out_specs=pl.BlockSpec((tm, tn), lambda i, j: (i, j)),
        out_shape=jax.ShapeDtypeStruct((m, n), BF16),
        compiler_params=_cparams("parallel", "parallel"),
        name="rope_q",
    )(cq, w_r, w_r_swapped, cos_t, sin_t)


ATTN_BLOCK = 256


def _attn_kernel(qn_ref, qr_ref, kv_ref, kr_ref, o_ref, *, seq):
    blk = ATTN_BLOCK
    nt = (((1,), (1,)), ((), ()))
    for h in range(2):
        qn_cols = slice(h * QK_NOPE_DIM, (h + 1) * QK_NOPE_DIM)
        qr_cols = slice(h * QK_ROPE_DIM, (h + 1) * QK_ROPE_DIM)
        kn_cols = slice(h * (QK_NOPE_DIM + V_HEAD_DIM), h * (QK_NOPE_DIM + V_HEAD_DIM) + QK_NOPE_DIM)
        v_cols = slice(h * (QK_NOPE_DIM + V_HEAD_DIM) + QK_NOPE_DIM, (h + 1) * (QK_NOPE_DIM + V_HEAD_DIM))
        o_cols = slice(h * V_HEAD_DIM, (h + 1) * V_HEAD_DIM)

        def q_body(qi, carry, qn_cols=qn_cols, qr_cols=qr_cols, kn_cols=kn_cols, v_cols=v_cols, o_cols=o_cols):
            q0 = pl.multiple_of(qi * blk, blk)
            qn = qn_ref[pl.ds(q0, blk), qn_cols]
            qr = qr_ref[pl.ds(q0, blk), qr_cols]

            def scores(k0):
                kn = kv_ref[pl.ds(k0, blk), kn_cols]
                krb = kr_ref[pl.ds(k0, blk), :]
                s = lax.dot_general(qn, kn, nt, preferred_element_type=F32)
                return s + lax.dot_general(qr, krb, nt, preferred_element_type=F32)

            def update(s, k0, m_i, l_i, acc):
                m_new = jnp.maximum(m_i, jnp.max(s, axis=-1, keepdims=True))
                p = jnp.exp(s - m_new)
                corr = jnp.exp(m_i - m_new)
                l_new = corr * l_i + jnp.sum(p, axis=-1, keepdims=True)
                v = kv_ref[pl.ds(k0, blk), v_cols]
                acc_new = corr * acc + jnp.dot(p.astype(BF16), v, preferred_element_type=F32)
                return m_new, l_new, acc_new

            def kv_body(kj, st):
                k0 = pl.multiple_of(kj * blk, blk)
                return update(scores(k0), k0, *st)

            init = (jnp.full((blk, 1), -jnp.inf, F32), jnp.zeros((blk, 1), F32),
                    jnp.zeros((blk, V_HEAD_DIM), F32))
            st = lax.fori_loop(0, qi, kv_body, init)
            s = scores(q0)
            row = lax.broadcasted_iota(jnp.int32, (blk, blk), 0)
            col = lax.broadcasted_iota(jnp.int32, (blk, blk), 1)
            s = jnp.where(col <= row, s, -jnp.inf)
            m_i, l_i, acc = update(s, q0, *st)
            o_ref[pl.ds(q0, blk), o_cols] = (acc / l_i).astype(o_ref.dtype)
            return carry

        lax.fori_loop(0, seq // blk, q_body, 0)


def _attention(qn, qr, kv, kr, *, batch, seq):
    t = qn.shape[0]
    hp = N_HEADS // 2
    kvw = 2 * (QK_NOPE_DIM + V_HEAD_DIM)
    return pl.pallas_call(
        functools.partial(_attn_kernel, seq=seq),
        grid=(batch, hp),
        in_specs=[
            pl.BlockSpec((seq, 2 * QK_NOPE_DIM), lambda b, g: (b, g)),
            pl.BlockSpec((seq, 2 * QK_ROPE_DIM), lambda b, g: (b, g)),
            pl.BlockSpec((seq, kvw), lambda b, g: (b, g)),
            pl.BlockSpec((seq, QK_ROPE_DIM), lambda b, g: (b, 0)),
        ],
        out_specs=pl.BlockSpec((seq, 2 * V_HEAD_DIM), lambda b, g: (b, g)),
        out_shape=jax.ShapeDtypeStruct((t, N_HEADS * V_HEAD_DIM), BF16),
        compiler_params=_cparams("parallel", "parallel"),
        name="attention",
    )(qn, qr, kv, kr)


def _split_hi_lo(w):
    hi = w.astype(BF16)
    lo = (w - hi.astype(F32)).astype(BF16)
    return hi, lo


def _swap_halves(w, width):
    k, n = w.shape
    return w.reshape(k, n // width, 2, width // 2)[:, :, ::-1, :].reshape(k, n)


def _moe_layer(x, logits_t, router_b, w_in, w_down, g, b):
    t, d = x.shape
    tm = MOE_ROW_TILE
    experts, gate_w, rank, counts = _route(logits_t, router_b)

    cnt = counts[:, 0].astype(jnp.int32)
    tiles_per_expert = (cnt + tm - 1) // tm
    tile_end = jnp.cumsum(tiles_per_expert)
    row_off = (tile_end - tiles_per_expert) * tm
    n_tiles = (2 * t) // tm + N_EXPERTS
    n_rows = n_tiles * tm
    tile_id = jnp.arange(n_tiles, dtype=jnp.int32)
    n_used = tile_end[-1].astype(jnp.int32)
    tile_valid = jnp.concatenate([(tile_id < n_used).astype(jnp.int32), n_used[None]])
    last_tile = jnp.maximum(n_used - 1, 0)
    tile_expert = jnp.searchsorted(tile_end, jnp.minimum(tile_id, last_tile), side="right").astype(jnp.int32)
    tile_expert = jnp.minimum(tile_expert, N_EXPERTS - 1)
    pos = row_off[experts] + rank
    tok = jnp.broadcast_to(jnp.arange(t, dtype=jnp.int32)[None, :], (2, t))
    src_tok = jnp.zeros((n_rows,), jnp.int32).at[pos.reshape(-1)].set(tok.reshape(-1))

    xs = _gather_rows(x, src_tok, out_dtype=BF16)
    hmid = _ffn_up(xs, w_in, tile_expert, tile_valid, tm=tm)
    ys = _ffn_down(hmid, w_down, tile_expert, tile_valid, tm=tm)
    return _combine_ln(ys, pos, gate_w.T, x, g, b)


def kernel(x, p, positions, conv_w_in, conv_b_in, conv_w_dw, conv_b_dw, conv_ln_g, conv_ln_b, conv_w_out, conv_b_out, kv_w_down, kv_norm_g, kv_w_up, q_w_down, q_norm_g, q_w_up, attn_w_out, router_w, router_b, moe_w_in, moe_w_down, ln1_g, ln1_b, ln2_g, ln2_b, ple_w_gate, ple_b_gate, ple_w_proj):
    batch, seq, d = x.shape
    t = batch * seq
    row = lambda v: v.reshape(1, -1).astype(F32)

    x0 = x.reshape(t, d)
    rw_hi, rw_lo = _split_hi_lo(router_w.T)
    pb = p.reshape(DEPTH, t, -1).astype(BF16)

    glu = _glu_matmul(x0.astype(BF16), conv_w_in[0].astype(BF16), row(conv_b_in[0]))
    hn = _conv_ln_swish(glu, conv_w_dw[0].reshape(CONV_KERNEL, d), row(conv_b_dw[0]),
                        row(conv_ln_g[0]), row(conv_ln_b[0]), batch=batch, seq=seq)
    x1, lg = _matmul_ln_router(hn, conv_w_out[0].astype(BF16), row(conv_b_out[0]), x0,
                               row(ln1_g[0]), row(ln1_b[0]), rw_hi, rw_lo)
    x2, x2b = _moe_layer(x1, lg, router_b, moe_w_in[0], moe_w_down[0], row(ln2_g[0]), row(ln2_b[0]))
    x3, x3b = _per_layer_input(x2b, x2, pb[0], ple_w_gate[0].astype(BF16), row(ple_b_gate[0]),
                               ple_w_proj[0].astype(BF16))

    cos_t, sin_t = _rope_tables(positions)
    qr_rank = q_w_down.shape[2]
    w_kpe = kv_w_down[:, KV_LORA_RANK:]
    w_cat = jnp.concatenate([q_w_down[0], kv_w_down[:, :KV_LORA_RANK], w_kpe,
                             _swap_halves(w_kpe, QK_ROPE_DIM)], axis=1).astype(BF16)
    cq, ckv, k_rot = _latents(x3b, w_cat, row(q_norm_g[0]), row(kv_norm_g), cos_t, sin_t, qr=qr_rank)
    kv = _matmul(ckv, kv_w_up.reshape(KV_LORA_RANK, -1).astype(BF16))

    w_qn = q_w_up[0][:, :, :QK_NOPE_DIM].reshape(qr_rank, -1).astype(BF16)
    w_qr = q_w_up[0][:, :, QK_NOPE_DIM:].reshape(qr_rank, -1)
    qn = _matmul(cq, w_qn, scale=SOFTMAX_SCALE)
    q_rot = _rope_q(cq, w_qr.astype(BF16), _swap_halves(w_qr, QK_ROPE_DIM).astype(BF16), cos_t, sin_t,
                    scale=SOFTMAX_SCALE)
    o = _attention(qn, q_rot, kv, k_rot, batch=batch, seq=seq)
    x4, lg = _matmul_ln_router(o, attn_w_out[0].astype(BF16), jnp.zeros((1, d), F32), x3,
                               row(ln1_g[1]), row(ln1_b[1]), rw_hi, rw_lo)
    x5, x5b = _moe_layer(x4, lg, router_b, moe_w_in[1], moe_w_down[1], row(ln2_g[1]), row(ln2_b[1]))
    x6, _ = _per_layer_input(x5b, x5, pb[1], ple_w_gate[1].astype(BF16), row(ple_b_gate[1]),
                             ple_w_proj[1].astype(BF16))
    return x6.reshape(batch, seq, d)
```

```python
import functools

import jax
import jax.numpy as jnp
from jax import lax
from jax.experimental import pallas as pl
from jax.experimental.pallas import tpu as pltpu

F32 = jnp.float32
BF16 = jnp.bfloat16

CONV_KERNEL = 31
N_HEADS = 32
QK_NOPE_DIM = 128
QK_ROPE_DIM = 64
V_HEAD_DIM = 128
KV_LORA_RANK = 512
ROPE_THETA = 10000.0
SOFTMAX_SCALE = (QK_NOPE_DIM + QK_ROPE_DIM) ** -0.5
N_EXPERTS = 32
N_GROUPS = 8
EXPERTS_PER_GROUP = N_EXPERTS // N_GROUPS
DEPTH = 2
ALPHA = (2 * DEPTH) ** 0.25
LN_EPS = 1e-5
RMS_EPS = 1e-6

LANES = 128
SUBLANES = 8
MOE_ROW_TILE = 512
VMEM_LIMIT = 56 * 1024 * 1024


def _cparams(*sem, vmem=VMEM_LIMIT):
    return pltpu.CompilerParams(dimension_semantics=sem, vmem_limit_bytes=vmem)


def _pick(n, pref):
    t = min(n, pref)
    while n % t:
        t //= 2
    return t


def _sigmoid(x):
    return 1.0 / (1.0 + jnp.exp(-x))


def _layer_norm(r, g, b):
    mu = jnp.mean(r, axis=-1, keepdims=True)
    rc = r - mu
    var = jnp.mean(rc * rc, axis=-1, keepdims=True)
    return rc * lax.rsqrt(var + LN_EPS) * g + b


def _mm_kernel(a_ref, w_ref, b_ref, o_ref, acc_ref, *, nk, scale):
    k = pl.program_id(2)

    @pl.when(k == 0)
    def _():
        acc_ref[...] = jnp.zeros_like(acc_ref)

    acc_ref[...] += jnp.dot(a_ref[...], w_ref[...], preferred_element_type=F32)

    @pl.when(k == nk - 1)
    def _():
        o_ref[...] = ((acc_ref[...] + b_ref[...]) * scale).astype(o_ref.dtype)


def _matmul(a, w, bias=None, *, scale=1.0, tm=1024, tn=1024, tk=1024):
    m, kd = a.shape
    n = w.shape[1]
    tm, tn, tk = _pick(m, tm), _pick(n, tn), _pick(kd, tk)
    if bias is None:
        bias = jnp.zeros((1, n), F32)
    nk = kd // tk
    return pl.pallas_call(
        functools.partial(_mm_kernel, nk=nk, scale=scale),
        grid=(m // tm, n // tn, nk),
        in_specs=[
            pl.BlockSpec((tm, tk), lambda i, j, k: (i, k)),
            pl.BlockSpec((tk, tn), lambda i, j, k: (k, j)),
            pl.BlockSpec((1, tn), lambda i, j, k: (0, j)),
        ],
        out_specs=pl.BlockSpec((tm, tn), lambda i, j, k: (i, j)),
        out_shape=jax.ShapeDtypeStruct((m, n), BF16),
        scratch_shapes=[pltpu.VMEM((tm, tn), F32)],
        compiler_params=_cparams("parallel", "parallel", "arbitrary"),
        name="matmul",
    )(a, w, bias)


def _glu_kernel(a_ref, wa_ref, wg_ref, ba_ref, bg_ref, o_ref, acca_ref, accg_ref, *, nk):
    k = pl.program_id(2)

    @pl.when(k == 0)
    def _():
        acca_ref[...] = jnp.zeros_like(acca_ref)
        accg_ref[...] = jnp.zeros_like(accg_ref)

    a = a_ref[...]
    acca_ref[...] += jnp.dot(a, wa_ref[...], preferred_element_type=F32)
    accg_ref[...] += jnp.dot(a, wg_ref[...], preferred_element_type=F32)

    @pl.when(k == nk - 1)
    def _():
        val = acca_ref[...] + ba_ref[...]
        gate = accg_ref[...] + bg_ref[...]
        o_ref[...] = (val * _sigmoid(gate)).astype(o_ref.dtype)


def _glu_matmul(a, w, bias, *, tm=1024, tn=512, tk=1024):
    m, kd = a.shape
    n = w.shape[1] // 2
    tm, tn, tk = _pick(m, tm), _pick(n, tn), _pick(kd, tk)
    nk, nj = kd // tk, n // tn
    return pl.pallas_call(
        functools.partial(_glu_kernel, nk=nk),
        grid=(m // tm, nj, nk),
        in_specs=[
            pl.BlockSpec((tm, tk), lambda i, j, k: (i, k)),
            pl.BlockSpec((tk, tn), lambda i, j, k: (k, j)),
            pl.BlockSpec((tk, tn), lambda i, j, k: (k, j + nj)),
            pl.BlockSpec((1, tn), lambda i, j, k: (0, j)),
            pl.BlockSpec((1, tn), lambda i, j, k: (0, j + nj)),
        ],
        out_specs=pl.BlockSpec((tm, tn), lambda i, j, k: (i, j)),
        out_shape=jax.ShapeDtypeStruct((m, n), F32),
        scratch_shapes=[pltpu.VMEM((tm, tn), F32), pltpu.VMEM((tm, tn), F32)],
        compiler_params=_cparams("parallel", "parallel", "arbitrary"),
        name="glu_matmul",
    )(a, w, w, bias, bias)


CONV_HALO = 32
CONV_ROWS = 64
CONV_LEAD = CONV_HALO - (CONV_KERNEL - 1)


def _conv_ln_kernel(prev_ref, cur_ref, w_ref, bdw_ref, g_ref, b_ref, o_ref, win_ref, conv_ref, *, ts, d):
    i = pl.program_id(1)

    @pl.when(i == 0)
    def _():
        win_ref[0:CONV_HALO, :] = jnp.zeros((CONV_HALO, d), F32)

    @pl.when(i > 0)
    def _():
        win_ref[0:CONV_HALO, :] = prev_ref[...]

    win_ref[CONV_HALO:CONV_HALO + ts, :] = cur_ref[...]
    win_ref[CONV_HALO + ts:, :] = jnp.zeros((SUBLANES, d), F32)

    def cols_body(cb, carry):
        cols = pl.ds(pl.multiple_of(cb * LANES, LANES), LANES)
        for r0 in range(0, ts, CONV_ROWS):
            acc = None
            for c in range(SUBLANES):
                u = None
                for a in range((CONV_LEAD + CONV_KERNEL - 1) // SUBLANES + 1):
                    k = SUBLANES * a + c - CONV_LEAD
                    if 0 <= k < CONV_KERNEL:
                        term = win_ref[pl.ds(r0 + SUBLANES * a, CONV_ROWS + SUBLANES), cols] * w_ref[k:k + 1, cols]
                        u = term if u is None else u + term
                part = u[c:c + CONV_ROWS]
                acc = part if acc is None else acc + part
            conv_ref[pl.ds(r0, CONV_ROWS), cols] = acc
        return carry

    lax.fori_loop(0, d // LANES, cols_body, 0)

    h = _layer_norm(conv_ref[...] + bdw_ref[...], g_ref[...], b_ref[...])
    o_ref[...] = (h * _sigmoid(h)).astype(o_ref.dtype)


def _conv_ln_swish(h, w_dw, b_dw, g, b, *, batch, seq, ts=256):
    t, d = h.shape
    ts = _pick(seq, ts)
    nblk = seq // ts
    hb = ts // CONV_HALO
    return pl.pallas_call(
        functools.partial(_conv_ln_kernel, ts=ts, d=d),
        grid=(batch, nblk),
        in_specs=[
            pl.BlockSpec((CONV_HALO, d), lambda bi, i: (jnp.maximum((bi * nblk + i) * hb - 1, 0), 0)),
            pl.BlockSpec((ts, d), lambda bi, i: (bi * nblk + i, 0)),
            pl.BlockSpec((CONV_KERNEL, d), lambda bi, i: (0, 0)),
            pl.BlockSpec((1, d), lambda bi, i: (0, 0)),
            pl.BlockSpec((1, d), lambda bi, i: (0, 0)),
            pl.BlockSpec((1, d), lambda bi, i: (0, 0)),
        ],
        out_specs=pl.BlockSpec((ts, d), lambda bi, i: (bi * nblk + i, 0)),
        out_shape=jax.ShapeDtypeStruct((t, d), BF16),
        scratch_shapes=[pltpu.VMEM((CONV_HALO + ts + SUBLANES, d), F32), pltpu.VMEM((ts, d), F32)],
        compiler_params=_cparams("parallel", "arbitrary"),
        name="conv_ln_swish",
    )(h, h, w_dw, b_dw, g, b)


def _slab_pitch(nslab):
    pitch = -(-nslab // SUBLANES) * SUBLANES
    return pitch if (pitch // SUBLANES) % 2 else pitch + SUBLANES


def _store_slabs(slab_ref, y, first_slab, tm, pitch):
    for j in range(y.shape[1] // LANES):
        slab_ref[pl.ds(first_slab + j, tm, stride=pitch), :] = y[:, j * LANES:(j + 1) * LANES]


def _zero_pad_slabs(slab_ref, nslab, tm, pitch):
    for j in range(nslab, pitch):
        slab_ref[pl.ds(j, tm, stride=pitch), :] = jnp.zeros((tm, LANES), F32)


def _load_slabs(slab_ref, base, first_slab, count, tm, pitch):
    parts = [slab_ref[pl.ds(base + first_slab + j, tm, stride=pitch), :] for j in range(count)]
    return parts[0] if count == 1 else jnp.concatenate(parts, axis=1)


def _mm_ln_kernel(a_ref, w_ref, bias_ref, res_ref, g_ref, b_ref, rwh_ref, rwl_ref,
                  x_ref, slab_ref, lg_ref, acc_ref, *, nk, tm, pitch):
    k = pl.program_id(1)

    @pl.when(k == 0)
    def _():
        acc_ref[...] = jnp.zeros_like(acc_ref)

    acc_ref[...] += jnp.dot(a_ref[...], w_ref[...], preferred_element_type=F32)

    @pl.when(k == nk - 1)
    def _():
        r = ALPHA * res_ref[...] + (acc_ref[...] + bias_ref[...])
        y = _layer_norm(r, g_ref[...], b_ref[...])
        x_ref[...] = y
        _store_slabs(slab_ref, y, 0, tm, pitch)
        _zero_pad_slabs(slab_ref, y.shape[1] // LANES, tm, pitch)
        yh = y.astype(BF16)
        yl =(y - yh.astype(F32)).astype(BF16)
        nt = (((1,), (1,)), ((), ()))
        lg = lax.dot_general(rwh_ref[...], yh, nt, preferred_element_type=F32)
        lg = lg + lax.dot_general(rwh_ref[...], yl, nt, preferred_element_type=F32)
        lg = lg + lax.dot_general(rwl_ref[...], yh, nt, preferred_element_type=F32)
        lg_ref[...] = lg


def _matmul_ln_router(a, w, bias, res, g, b, rw_hi, rw_lo, *, tm=256, tk=512):
    m, kd = a.shape
    d = w.shape[1]
    tm, tk = _pick(m, tm), _pick(kd, tk)
    nk = kd // tk
    ne = rw_hi.shape[0]
    pitch = _slab_pitch(d // LANES)
    row = lambda i, k: (i, 0)
    const = lambda i, k: (0, 0)
    return pl.pallas_call(
        functools.partial(_mm_ln_kernel, nk=nk, tm=tm, pitch=pitch),
        grid=(m // tm, nk),
        in_specs=[
            pl.BlockSpec((tm, tk), lambda i, k: (i, k)),
            pl.BlockSpec((tk, d), lambda i, k: (k, 0)),
            pl.BlockSpec((1, d), const),
            pl.BlockSpec((tm, d), row),
            pl.BlockSpec((1, d), const),
            pl.BlockSpec((1, d), const),
            pl.BlockSpec((ne, d), const),
            pl.BlockSpec((ne, d), const),
        ],
        out_specs=[
            pl.BlockSpec((tm, d), row),
            pl.BlockSpec((tm * pitch, LANES), row),
            pl.BlockSpec((ne, tm), lambda i, k: (0, i)),
        ],
        out_shape=[
            jax.ShapeDtypeStruct((m, d), F32),
            jax.ShapeDtypeStruct((m * pitch, LANES), F32),
            jax.ShapeDtypeStruct((ne, m), F32),
        ],
        scratch_shapes=[pltpu.VMEM((tm, d), F32)],
        compiler_params=_cparams("parallel", "arbitrary"),
        name="matmul_ln_router",
    )(a, w, bias, res, g, b, rw_hi, rw_lo)


def _route_kernel(lg_ref, rb_ref, e_ref, w_ref, rank_ref, cnt_ref, carry_ref, *, tt, nsteps):
    step = pl.program_id(0)

    @pl.when(step == 0)
    def _():
        carry_ref[...] = jnp.zeros_like(carry_ref)

    aff = _sigmoid(lg_ref[...])
    biased = aff + rb_ref[...]
    rows_b = [biased[e:e + 1, :] for e in range(N_EXPERTS)]
    rows_a = [aff[e:e + 1, :] for e in range(N_EXPERTS)]

    def top2_sum(v):
        m01, n01 = jnp.maximum(v[0], v[1]), jnp.minimum(v[0], v[1])
        m23, n23 = jnp.maximum(v[2], v[3]), jnp.minimum(v[2], v[3])
        return jnp.maximum(m01, m23) + jnp.maximum(jnp.minimum(m01, m23), jnp.maximum(n01, n23))

    best = top2_sum(rows_b[0:EXPERTS_PER_GROUP])
    gidx = jnp.zeros((1, tt), jnp.int32)
    for g in range(1, N_GROUPS):
        s = top2_sum(rows_b[g * EXPERTS_PER_GROUP:(g + 1) * EXPERTS_PER_GROUP])
        better = s > best
        gidx = jnp.where(better, g, gidx)
        best = jnp.where(better, s, best)

    vb = [rows_b[i] for i in range(EXPERTS_PER_GROUP)]
    va = [rows_a[i] for i in range(EXPERTS_PER_GROUP)]
    for g in range(1, N_GROUPS):
        sel = gidx == g
        for i in range(EXPERTS_PER_GROUP):
            vb[i] = jnp.where(sel, rows_b[g * EXPERTS_PER_GROUP + i], vb[i])
            va[i] = jnp.where(sel, rows_a[g * EXPERTS_PER_GROUP + i], va[i])

    i1 = jnp.zeros((1, tt), jnp.int32)
    b1, a1 = vb[0], va[0]
    for i in range(1, EXPERTS_PER_GROUP):
        better = vb[i] > b1
        i1 = jnp.where(better, i, i1)
        b1 = jnp.where(better, vb[i], b1)
        a1 = jnp.where(better, va[i], a1)
    i2 = jnp.where(i1 == 0, 1, 0).astype(jnp.int32)
    b2 = jnp.where(i1 == 0, vb[1], vb[0])
    a2 = jnp.where(i1 == 0, va[1], va[0])
    for i in range(1, EXPERTS_PER_GROUP):
        better = jnp.logical_and(vb[i] > b2, i1 != i)
        i2 = jnp.where(better, i, i2)
        b2 = jnp.where(better, vb[i], b2)
        a2 = jnp.where(better, va[i], a2)

    e1 = gidx * EXPERTS_PER_GROUP + i1
    e2 = gidx * EXPERTS_PER_GROUP + i2
    den = a1 + a2
    e_ref[0:1, :] = e1
    e_ref[1:2, :] = e2
    w_ref[0:1, :] = a1 / den
    w_ref[1:2, :] = a2 / den

    eid = lax.broadcasted_iota(jnp.int32, (N_EXPERTS, tt), 0)
    hit1 = eid == e1
    hit2 = eid == e2
    onehot = jnp.where(jnp.logical_or(hit1, hit2), 1.0, 0.0)
    src = lax.broadcasted_iota(jnp.int32, (tt, tt), 0)
    dst = lax.broadcasted_iota(jnp.int32, (tt, tt), 1)
    upper = jnp.where(src < dst, 1.0, 0.0).astype(BF16)
    before = jnp.dot(onehot.astype(BF16), upper, preferred_element_type=F32) + carry_ref[:, 0:1]
    rank_ref[0:1, :] = jnp.sum(jnp.where(hit1, before, 0.0), axis=0, keepdims=True).astype(jnp.int32)
    rank_ref[1:2, :] = jnp.sum(jnp.where(hit2, before, 0.0), axis=0, keepdims=True).astype(jnp.int32)
    carry_ref[...] += jnp.sum(onehot, axis=1, keepdims=True)

    @pl.when(step == nsteps - 1)
    def _():
        cnt_ref[...] = carry_ref[...]


def _route(logits_t, router_b, *, tt=512):
    ne, t = logits_t.shape
    tt = _pick(t, tt)
    nsteps = t // tt
    return pl.pallas_call(
        functools.partial(_route_kernel, tt=tt, nsteps=nsteps),
        grid=(nsteps,),
        in_specs=[
            pl.BlockSpec((ne, tt), lambda s: (0, s)),
            pl.BlockSpec((ne, 1), lambda s: (0, 0)),
        ],
        out_specs=[
            pl.BlockSpec((2, tt), lambda s: (0, s)),
            pl.BlockSpec((2, tt), lambda s: (0, s)),
            pl.BlockSpec((2, tt), lambda s: (0, s)),
            pl.BlockSpec((ne, LANES), lambda s: (0, 0)),
        ],
        out_shape=[
            jax.ShapeDtypeStruct((2, t), jnp.int32),
            jax.ShapeDtypeStruct((2, t), F32),
            jax.ShapeDtypeStruct((2, t), jnp.int32),
            jax.ShapeDtypeStruct((ne, LANES), F32),
        ],
        scratch_shapes=[pltpu.VMEM((ne, LANES), F32)],
        compiler_params=_cparams("arbitrary"),
        name="route",
    )(logits_t, router_b.reshape(ne, 1).astype(F32))


def _token_copy(src_ref, dst_ref, sem, src_tok, dst_tok, nslab, pitch):
    src = src_ref.at[pl.ds(pl.multiple_of(src_tok * pitch, SUBLANES), nslab)]
    dst = dst_ref.at[pl.ds(pl.multiple_of(dst_tok * pitch, SUBLANES), nslab)]
    return pltpu.make_async_copy(src, dst, sem)


def _ffn_up_kernel(te_ref, tv_ref, cur_ref, nxt_ref, slab_ref, wg_ref, wu_ref, o_ref,
                   buf_ref, sem, accg_ref, accu_ref, *, nk, tm, n_tiles, nslab, pitch):
    i = pl.program_id(0)
    k = pl.program_id(1)
    n_used = tv_ref[n_tiles]
    used = i < n_used
    slot = lax.rem(i, 2)
    spk = nslab // nk
    per_step = tm // nk

    def start(idx_ref, dst_slot, row):
        _token_copy(slab_ref, buf_ref, sem.at[dst_slot], idx_ref[0, 0, row], dst_slot * tm + row,
                    nslab, pitch).start()

    @pl.when(jnp.logical_and(used, jnp.logical_and(i == 0, k == 0)))
    def _():
        def body(r, carry):
            start(cur_ref, 0, r)
            return carry

        lax.fori_loop(0, tm, body, 0)

    @pl.when(jnp.logical_and(used, k == 0))
    def _():
        def body(r, carry):
            _token_copy(slab_ref, buf_ref, sem.at[slot], 0, slot * tm + r, nslab, pitch).wait()
            return carry

        lax.fori_loop(0, tm, body, 0)
        accg_ref[...] = jnp.zeros_like(accg_ref)
        accu_ref[...] = jnp.zeros_like(accu_ref)

    def step():
        x = _load_slabs(buf_ref, slot * (tm * pitch), k * spk, spk, tm, pitch).astype(BF16)
        accg_ref[...] += jnp.dot(x, wg_ref[...].astype(BF16), preferred_element_type=F32)
        accu_ref[...] += jnp.dot(x, wu_ref[...].astype(BF16), preferred_element_type=F32)

    @pl.when(jnp.logical_and(used, i + 1 < n_used))
    def _():
        for r in range(per_step):
            start(nxt_ref, 1 - slot, k * per_step + r)
        step()

    @pl.when(jnp.logical_and(used, i + 1 >= n_used))
    def _():
        step()

    @pl.when(jnp.logical_and(used, k == nk - 1))
    def _():
        hg = accg_ref[...]
        o_ref[...] = (hg * _sigmoid(hg) * accu_ref[...]).astype(o_ref.dtype)

    @pl.when(jnp.logical_and(jnp.logical_not(used), k == nk - 1))
    def _():
        o_ref[...] = jnp.zeros_like(o_ref)


def _ffn_up(slabs, src_tok, w_in, layer, tile_expert, tile_valid, *, tm, tk=512):
    d = w_in.shape[2]
    de = w_in.shape[3] // 2
    nslab = d // LANES
    pitch = _slab_pitch(nslab)
    tk = _pick(d, tk)
    nk = d // tk
    n_tiles = src_tok.shape[0] // tm
    idx = src_tok.reshape(n_tiles, 1, tm)

    def kk(i, k, tv):
        return k * tv[i] + (nk - 1) * (1 - tv[i])

    grid_spec = pltpu.PrefetchScalarGridSpec(
        num_scalar_prefetch=2,
        grid=(n_tiles, nk),
        in_specs=[
            pl.BlockSpec((1, 1, tm), lambda i, k, te, tv: (i, 0, 0), memory_space=pltpu.SMEM),
            pl.BlockSpec((1, 1, tm), lambda i, k, te, tv: (jnp.minimum(i + 1, n_tiles - 1), 0, 0),
                         memory_space=pltpu.SMEM),
            pl.BlockSpec(memory_space=pl.ANY),
            pl.BlockSpec((None, None, tk, de), lambda i, k, te, tv: (layer, te[i], kk(i, k, tv), 0)),
            pl.BlockSpec((None, None, tk, de), lambda i, k, te, tv: (layer, te[i], kk(i, k, tv), 1)),
        ],
        out_specs=pl.BlockSpec((tm, de), lambda i, k, te, tv: (i, 0)),
        scratch_shapes=[
            pltpu.VMEM((2 * tm * pitch, LANES), F32),
            pltpu.SemaphoreType.DMA((2,)),
            pltpu.VMEM((tm, de), F32),
            pltpu.VMEM((tm, de), F32),
        ],
    )
    return pl.pallas_call(
        functools.partial(_ffn_up_kernel, nk=nk, tm=tm, n_tiles=n_tiles, nslab=nslab, pitch=pitch),
        grid_spec=grid_spec,
        out_shape=jax.ShapeDtypeStruct((n_tiles * tm, de), BF16),
        compiler_params=_cparams("arbitrary", "arbitrary"),
        name="ffn_up",
    )(tile_expert, tile_valid, idx, idx, slabs, w_in, w_in)


def _ffn_down_kernel(te_ref, tv_ref, h_ref, w_ref, o_ref, *, tm, nslab, pitch, spj):
    i = pl.program_id(0)
    j = pl.program_id(1)
    valid = tv_ref[i] > 0

    @pl.when(valid)
    def _():
        y = jnp.dot(h_ref[...], w_ref[...].astype(BF16), preferred_element_type=F32)
        _store_slabs(o_ref, y, j * spj, tm, pitch)

    @pl.when(jnp.logical_and(valid, j == 0))
    def _():
        _zero_pad_slabs(o_ref, nslab, tm, pitch)

    @pl.when(jnp.logical_and(jnp.logical_not(valid), j == 0))
    def _():
        o_ref[...] = jnp.zeros_like(o_ref)


def _ffn_down(h, w_down, layer, tile_expert, tile_valid, *, tm, tn=1024):
    r, de = h.shape
    d = w_down.shape[3]
    nslab = d // LANES
    pitch = _slab_pitch(nslab)
    tn = _pick(d, tn)
    nj = d // tn
    n_tiles = r // tm

    def jj(i, j, tv):
        return j * tv[i] + (nj - 1) * (1 - tv[i])

    def ii(i, tv):
        return jnp.maximum(jnp.minimum(i, tv[n_tiles] - 1), 0)

    grid_spec = pltpu.PrefetchScalarGridSpec(
        num_scalar_prefetch=2,
        grid=(n_tiles, nj),
        in_specs=[
            pl.BlockSpec((tm, de), lambda i, j, te, tv: (ii(i, tv), 0)),
            pl.BlockSpec((None, None, de, tn), lambda i, j, te, tv: (layer, te[i], 0, jj(i, j, tv))),
        ],
        out_specs=pl.BlockSpec((tm * pitch, LANES), lambda i, j, te, tv: (i, 0)),
    )
    return pl.pallas_call(
        functools.partial(_ffn_down_kernel, tm=tm, nslab=nslab, pitch=pitch, spj=tn // LANES),
        grid_spec=grid_spec,
        out_shape=jax.ShapeDtypeStruct((r * pitch, LANES), F32),
        compiler_params=_cparams("arbitrary", "arbitrary"),
        name="ffn_down",
    )(tile_expert, tile_valid, h, w_down)


def _combine_ln_kernel(cur_ref, nxt_ref, ys_ref, gw_ref, x_ref, g_ref, b_ref, xo_ref, xb_ref, buf_ref, sem,
                       *, tm, nb, nslab, pitch):
    i = pl.program_id(0)
    slot = lax.rem(i, 2)

    def copies(idx_ref, dst_slot, r):
        first = _token_copy(ys_ref, buf_ref, sem.at[dst_slot], idx_ref[0, 0, r],
                            (2 * dst_slot) * tm + r, nslab, pitch)
        second = _token_copy(ys_ref, buf_ref, sem.at[dst_slot], idx_ref[0, 0, tm + r],
                             (2 * dst_slot + 1) * tm + r, nslab, pitch)
        return first, second

    def issue(idx_ref, dst_slot):
        def body(r, carry):
            first, second = copies(idx_ref, dst_slot, r)
            first.start()
            second.start()
            return carry

        lax.fori_loop(0, tm, body, 0)

    @pl.when(i == 0)
    def _():
        issue(cur_ref, 0)

    @pl.when(i + 1 < nb)
    def _():
        issue(nxt_ref, 1 - slot)

    def drain(r, carry):
        first, second = copies(cur_ref, slot, r)
        first.wait()
        second.wait()
        return carry

    lax.fori_loop(0, tm, drain, 0)
    gw = gw_ref[...]
    y0 = _load_slabs(buf_ref, (2 * slot) * (tm * pitch), 0, nslab, tm, pitch)
    y1 = _load_slabs(buf_ref, (2 * slot + 1) * (tm * pitch), 0, nslab, tm, pitch)
    y = gw[:, 0:1] * y0 + gw[:, 1:2] * y1
    out = _layer_norm(ALPHA * x_ref[...] + y, g_ref[...], b_ref[...])
    xo_ref[...] = out
    xb_ref[...] = out.astype(BF16)


def _combine_ln(ys, pos, gate_w, x, g, b, *, tm=256):
    t, d = x.shape
    tm = _pick(t, tm)
    nb = t // tm
    nslab = d // LANES
    pitch = _slab_pitch(nslab)
    pos_tiles = pos.reshape(2, nb, tm).transpose(1, 0, 2).reshape(nb, 1, 2 * tm)
    row = lambda i: (i, 0)
    const = lambda i: (0, 0)
    return pl.pallas_call(
        functools.partial(_combine_ln_kernel, tm=tm, nb=nb, nslab=nslab, pitch=pitch),
        grid=(nb,),
        in_specs=[
            pl.BlockSpec((1, 1, 2 * tm), lambda i: (i, 0, 0), memory_space=pltpu.SMEM),
            pl.BlockSpec((1, 1, 2 * tm), lambda i: (jnp.minimum(i + 1, nb - 1), 0, 0), memory_space=pltpu.SMEM),
            pl.BlockSpec(memory_space=pl.ANY),
            pl.BlockSpec((tm, 2), row),
            pl.BlockSpec((tm, d), row),
            pl.BlockSpec((1, d), const),
            pl.BlockSpec((1, d), const),
        ],
        out_specs=[pl.BlockSpec((tm, d), row), pl.BlockSpec((tm, d), row)],
        out_shape=[jax.ShapeDtypeStruct((t, d), F32), jax.ShapeDtypeStruct((t, d), BF16)],
        scratch_shapes=[pltpu.VMEM((4 * tm * pitch, LANES), F32), pltpu.SemaphoreType.DMA((2,))],
        compiler_params=_cparams("arbitrary"),
        name="combine_ln",
    )(pos_tiles, pos_tiles, ys, gate_w, x, g, b)


def _ple_kernel(a_ref, wg_ref, bg_ref, p_ref, wp_ref, x_ref, xo_ref, xb_ref, acc_ref, *, nk):
    k = pl.program_id(2)

    @pl.when(k == 0)
    def _():
        acc_ref[...] = jnp.zeros_like(acc_ref)

    acc_ref[...] += jnp.dot(a_ref[...], wg_ref[...], preferred_element_type=F32)

    @pl.when(k == nk - 1)
    def _():
        gate = _sigmoid(acc_ref[...] + bg_ref[...])
        proj = jnp.dot(p_ref[...], wp_ref[...], preferred_element_type=F32)
        out = x_ref[...] + gate * proj
        xo_ref[...] = out
        xb_ref[...] = out.astype(BF16)


def _per_layer_input(xb, x, p, w_gate, b_gate, w_proj, *, tm=1024, tn=1024, tk=1024):
    m, d = x.shape
    pd = p.shape[1]
    tm, tn, tk = _pick(m, tm), _pick(d, tn), _pick(d, tk)
    nk = d // tk
    tile = lambda i, j, k: (i, j)
    return pl.pallas_call(
        functools.partial(_ple_kernel, nk=nk),
        grid=(m // tm, d // tn, nk),
        in_specs=[
            pl.BlockSpec((tm, tk), lambda i, j, k: (i, k)),
            pl.BlockSpec((tk, tn), lambda i, j, k: (k, j)),
            pl.BlockSpec((1, tn), lambda i, j, k: (0, j)),
            pl.BlockSpec((tm, pd), lambda i, j, k: (i, 0)),
            pl.BlockSpec((pd, tn), lambda i, j, k: (0, j)),
            pl.BlockSpec((tm, tn), tile),
        ],
        out_specs=[pl.BlockSpec((tm, tn), tile), pl.BlockSpec((tm, tn), tile)],
        out_shape=[jax.ShapeDtypeStruct((m, d), F32), jax.ShapeDtypeStruct((m, d), BF16)],
        scratch_shapes=[pltpu.VMEM((tm, tn), F32)],
        compiler_params=_cparams("parallel", "parallel", "arbitrary"),
        name="per_layer_input",
    )(xb, w_gate, b_gate, p, w_proj, x)


def _rope_table_kernel(pos_ref, cos_ref, sin_ref):
    half = QK_ROPE_DIM // 2
    lane = lax.broadcasted_iota(jnp.int32, (1, LANES), 1)
    freq = (lane % half).astype(F32)
    inv_freq = 1.0 / (ROPE_THETA ** (freq * (2.0 / QK_ROPE_DIM)))
    ang = pos_ref[...].astype(F32) * inv_freq
    sign = jnp.where((lane // half) % 2 == 0, -1.0, 1.0)
    cos_ref[...] = jnp.cos(ang)
    sin_ref[...] = jnp.sin(ang) * sign


def _rope_tables(positions, *, tm=2048):
    t = positions.size
    tm = _pick(t, tm)
    return pl.pallas_call(
        _rope_table_kernel,
        grid=(t // tm,),
        in_specs=[pl.BlockSpec((tm, 1), lambda i: (i, 0))],
        out_specs=[pl.BlockSpec((tm, LANES), lambda i: (i, 0))] * 2,
        out_shape=[jax.ShapeDtypeStruct((t, LANES), F32)] * 2,
        compiler_params=_cparams("parallel"),
        name="rope_tables",
    )(positions.reshape(t, 1))


def _latent_kernel(a_ref, w_ref, gq_ref, gkv_ref, cos_ref, sin_ref, cq_ref, ckv_ref, kr_ref, acc_ref, *, nk, qr):
    k = pl.program_id(1)

    @pl.when(k == 0)
    def _():
        acc_ref[...] = jnp.zeros_like(acc_ref)

    acc_ref[...] += jnp.dot(a_ref[...], w_ref[...], preferred_element_type=F32)

    @pl.when(k == nk - 1)
    def _():
        c = acc_ref[...]
        cq = c[:, :qr]
        cq_ref[...] = (cq * lax.rsqrt(jnp.mean(cq * cq, axis=-1, keepdims=True) + RMS_EPS)
                       * gq_ref[...]).astype(BF16)
        ckv = c[:, qr:qr + KV_LORA_RANK]
        ckv_ref[...] = (ckv * lax.rsqrt(jnp.mean(ckv * ckv, axis=-1, keepdims=True) + RMS_EPS)
                        * gkv_ref[...]).astype(BF16)
        kpe = c[:, qr + KV_LORA_RANK:]
        cos = cos_ref[...]
        sin = sin_ref[...]
        rot = kpe[:, :QK_ROPE_DIM] * cos[:, :QK_ROPE_DIM] + kpe[:, QK_ROPE_DIM:] * sin[:, :QK_ROPE_DIM]
        kr_ref[...] = rot.astype(BF16)


def _latents(xb, w_cat, gq, gkv, cos_t, sin_t, *, qr, tm=512, tk=512):
    m, kd = xb.shape
    n = w_cat.shape[1]
    tm, tk = _pick(m, tm), _pick(kd, tk)
    nk = kd // tk
    row = lambda i, k: (i, 0)
    const = lambda i, k: (0, 0)
    return pl.pallas_call(
        functools.partial(_latent_kernel, nk=nk, qr=qr),
        grid=(m // tm, nk),
        in_specs=[
            pl.BlockSpec((tm, tk), lambda i, k: (i, k)),
            pl.BlockSpec((tk, n), lambda i, k: (k, 0)),
            pl.BlockSpec((1, qr), const),
            pl.BlockSpec((1, KV_LORA_RANK), const),
            pl.BlockSpec((tm, LANES), row),
            pl.BlockSpec((tm, LANES), row),
        ],
        out_specs=[
            pl.BlockSpec((tm, qr), row),
            pl.BlockSpec((tm, KV_LORA_RANK), row),
            pl.BlockSpec((tm, QK_ROPE_DIM), row),
        ],
        out_shape=[
            jax.ShapeDtypeStruct((m, qr), BF16),
            jax.ShapeDtypeStruct((m, KV_LORA_RANK), BF16),
            jax.ShapeDtypeStruct((m, QK_ROPE_DIM), BF16),
        ],
        scratch_shapes=[pltpu.VMEM((tm, n), F32)],
        compiler_params=_cparams("parallel", "arbitrary"),
        name="latents",
    )(xb, w_cat, gq, gkv, cos_t, sin_t)


def _rope_q_kernel(a_ref, w_ref, ws_ref, cos_ref, sin_ref, o_ref, *, scale, reps):
    a = a_ref[...]
    q = jnp.dot(a, w_ref[...], preferred_element_type=F32)
    qs = jnp.dot(a, ws_ref[...], preferred_element_type=F32)
    cos = jnp.tile(cos_ref[...], (1, reps))
    sin = jnp.tile(sin_ref[...], (1, reps))
    o_ref[...] = ((q * cos + qs * sin) * scale).astype(o_ref.dtype)


def _rope_q(cq, w_r, w_r_swapped, cos_t, sin_t, *, scale, tm=1024, tn=1024):
    m, kd = cq.shape
    n = w_r.shape[1]
    tm, tn = _pick(m, tm), _pick(n, tn)
    return pl.pallas_call(
        functools.partial(_rope_q_kernel, scale=scale, reps=tn // LANES),
        grid=(m // tm, n // tn),
        in_specs=[
            pl.BlockSpec((tm, kd), lambda i, j: (i, 0)),
            pl.BlockSpec((kd, tn), lambda i, j: (0, j)),
            pl.BlockSpec((kd, tn), lambda i, j: (0, j)),
            pl.BlockSpec((tm, LANES), lambda i, j: (i, 0)),
            pl.BlockSpec((tm, LANES), lambda i, j: (i, 0)),
        ],
        out_specs=pl.BlockSpec((tm, tn), lambda i, j: (i, j)),
        out_shape=jax.ShapeDtypeStruct((m, n), BF16),
        compiler_params=_cparams("parallel", "parallel"),
        name="rope_q",
    )(cq, w_r, w_r_swapped, cos_t, sin_t)


ATTN_BLOCK = 256


def _attn_kernel(qn_ref, qr_ref, kv_ref, kr_ref, o_ref, qc_ref, kc_ref, s_ref, p_ref, *, seq):
    blk = ATTN_BLOCK
    nt = (((1,), (1,)), ((), ()))
    qk = QK_NOPE_DIM + QK_ROPE_DIM
    hw = QK_NOPE_DIM + V_HEAD_DIM
    row = lax.broadcasted_iota(jnp.int32, (blk, blk), 0)
    col = lax.broadcasted_iota(jnp.int32, (blk, blk), 1)
    for h in range(2):
        qc_ref[:, :QK_NOPE_DIM] = qn_ref[:, h * QK_NOPE_DIM:(h + 1) * QK_NOPE_DIM]
        qc_ref[:, QK_NOPE_DIM:qk] = qr_ref[:, h * QK_ROPE_DIM:(h + 1) * QK_ROPE_DIM]
        kc_ref[:, :QK_NOPE_DIM] = kv_ref[:, h * hw:h * hw + QK_NOPE_DIM]
        kc_ref[:, QK_NOPE_DIM:qk] = kr_ref[...]
        for i in range(seq // blk):
            par = i % 2
            q = qc_ref[i * blk:(i + 1) * blk, :]
            m = None
            for j in range(i + 1):
                s = lax.dot_general(q, kc_ref[j * blk:(j + 1) * blk, :], nt, preferred_element_type=F32)
                if j == i:
                    s = jnp.where(col <= row, s, -jnp.inf)
                s_ref[par, :, j * blk:(j + 1) * blk] = s
                mj = jnp.max(s, axis=-1, keepdims=True)
                m = mj if m is None else jnp.maximum(m, mj)
            l = None
            for j in range(i + 1):
                p = jnp.exp(s_ref[par, :, j * blk:(j + 1) * blk] - m)
                lj = jnp.sum(p, axis=-1, keepdims=True)
                l = lj if l is None else l + lj
                p_ref[par, :, j * blk:(j + 1) * blk] = p.astype(BF16)
            w = (i + 1) * blk
            v = kv_ref[0:w, h * hw + QK_NOPE_DIM:(h + 1) * hw]
            acc = jnp.dot(p_ref[par, :, 0:w], v, preferred_element_type=F32)
            o_ref[i * blk:(i + 1) * blk, h * V_HEAD_DIM:(h + 1) * V_HEAD_DIM] = (acc / l).astype(o_ref.dtype)


def _attention(qn, qr, kv, kr, *, batch, seq):
    t = qn.shape[0]
    hp = N_HEADS // 2
    kvw = 2 * (QK_NOPE_DIM + V_HEAD_DIM)
    qk = QK_NOPE_DIM + QK_ROPE_DIM
    return pl.pallas_call(
        functools.partial(_attn_kernel, seq=seq),
        grid=(batch, hp),
        in_specs=[
            pl.BlockSpec((seq, 2 * QK_NOPE_DIM), lambda b, g: (b, g)),
            pl.BlockSpec((seq, 2 * QK_ROPE_DIM), lambda b, g: (b, g)),
            pl.BlockSpec((seq, kvw), lambda b, g: (b, g)),
            pl.BlockSpec((seq, QK_ROPE_DIM), lambda b, g: (b, 0)),
        ],
        out_specs=pl.BlockSpec((seq, 2 * V_HEAD_DIM), lambda b, g: (b, g)),
        out_shape=jax.ShapeDtypeStruct((t, N_HEADS * V_HEAD_DIM), BF16),
        scratch_shapes=[
            pltpu.VMEM((seq, qk), BF16),
            pltpu.VMEM((seq, qk), BF16),
            pltpu.VMEM((2, ATTN_BLOCK, seq), F32),
            pltpu.VMEM((2, ATTN_BLOCK, seq), BF16),
        ],
        compiler_params=_cparams("parallel", "parallel"),
        name="attention",
    )(qn, qr, kv, kr)


def _split_hi_lo(w):
    hi = w.astype(BF16)
    lo = (w - hi.astype(F32)).astype(BF16)
    return hi, lo


def _swap_halves(w, width):
    k, n = w.shape
    return w.reshape(k, n // width, 2, width // 2)[:, :, ::-1, :].reshape(k, n)


def _moe_layer(x, x_slabs, logits_t, router_b, w_in, w_down, layer, g, b):
    t, d = x.shape
    tm = MOE_ROW_TILE
    experts, gate_w, rank, counts = _route(logits_t, router_b)

    cnt = counts[:, 0].astype(jnp.int32)
    tiles_per_expert = (cnt + tm - 1) // tm
    tile_end = jnp.cumsum(tiles_per_expert)
    row_off = (tile_end - tiles_per_expert) * tm
    n_tiles = (2 * t) // tm + N_EXPERTS
    n_rows = n_tiles * tm
    tile_id = jnp.arange(n_tiles, dtype=jnp.int32)
    n_used = tile_end[-1].astype(jnp.int32)
    tile_valid = jnp.concatenate([(tile_id < n_used).astype(jnp.int32), n_used[None]])
    last_tile = jnp.maximum(n_used - 1, 0)
    tile_expert = jnp.searchsorted(tile_end, jnp.minimum(tile_id, last_tile), side="right").astype(jnp.int32)
    tile_expert = jnp.minimum(tile_expert, N_EXPERTS - 1)
    onehot = experts[:, :, None] == jnp.arange(N_EXPERTS, dtype=jnp.int32)
    pos = jnp.sum(jnp.where(onehot, row_off, 0), axis=-1) + rank
    tok = jnp.broadcast_to(jnp.arange(t, dtype=jnp.int32)[None, :], (2, t))
    src_tok = jnp.zeros((n_rows,), jnp.int32).at[pos.reshape(-1)].set(tok.reshape(-1))

    hmid = _ffn_up(x_slabs, src_tok, w_in, layer, tile_expert, tile_valid, tm=tm)
    ys = _ffn_down(hmid, w_down, layer, tile_expert, tile_valid, tm=tm)
    return _combine_ln(ys, pos, gate_w.T, x, g, b)


def kernel(x, p, positions, conv_w_in, conv_b_in, conv_w_dw, conv_b_dw, conv_ln_g, conv_ln_b, conv_w_out, conv_b_out, kv_w_down, kv_norm_g, kv_w_up, q_w_down, q_norm_g, q_w_up, attn_w_out, router_w, router_b, moe_w_in, moe_w_down, ln1_g, ln1_b, ln2_g, ln2_b, ple_w_gate, ple_b_gate, ple_w_proj):
    batch, seq, d = x.shape
    t = batch * seq
    row = lambda v: v.reshape(1, -1).astype(F32)

    x0 = x.reshape(t, d)
    rw_hi, rw_lo = _split_hi_lo(router_w.T)
    pb = p.reshape(DEPTH, t, -1).astype(BF16)

    glu = _glu_matmul(x0.astype(BF16), conv_w_in[0].astype(BF16), row(conv_b_in[0]))
    hn = _conv_ln_swish(glu, conv_w_dw[0].reshape(CONV_KERNEL, d), row(conv_b_dw[0]),
                        row(conv_ln_g[0]), row(conv_ln_b[0]), batch=batch, seq=seq)
    x1, x1s, lg = _matmul_ln_router(hn, conv_w_out[0].astype(BF16), row(conv_b_out[0]), x0,
                                    row(ln1_g[0]), row(ln1_b[0]), rw_hi, rw_lo)
    x2, x2b = _moe_layer(x1, x1s, lg, router_b, moe_w_in, moe_w_down, 0, row(ln2_g[0]), row(ln2_b[0]))
    x3, x3b = _per_layer_input(x2b, x2, pb[0], ple_w_gate[0].astype(BF16), row(ple_b_gate[0]),
                               ple_w_proj[0].astype(BF16))

    cos_t, sin_t = _rope_tables(positions)
    qr_rank = q_w_down.shape[2]
    w_kpe = kv_w_down[:, KV_LORA_RANK:]
    w_cat = jnp.concatenate([q_w_down[0], kv_w_down[:, :KV_LORA_RANK], w_kpe,
                             _swap_halves(w_kpe, QK_ROPE_DIM)], axis=1).astype(BF16)
    cq, ckv, k_rot = _latents(x3b, w_cat, row(q_norm_g[0]), row(kv_norm_g), cos_t, sin_t, qr=qr_rank)
    kv = _matmul(ckv, kv_w_up.reshape(KV_LORA_RANK, -1).astype(BF16))

    w_qn = q_w_up[0][:, :, :QK_NOPE_DIM].reshape(qr_rank, -1).astype(BF16)
    w_qr = q_w_up[0][:, :, QK_NOPE_DIM:].reshape(qr_rank, -1)
    qn = _matmul(cq, w_qn, scale=SOFTMAX_SCALE)
    q_rot = _rope_q(cq, w_qr.astype(BF16), _swap_halves(w_qr, QK_ROPE_DIM).astype(BF16), cos_t, sin_t,
                    scale=SOFTMAX_SCALE)
    o = _attention(qn, q_rot, kv, k_rot, batch=batch, seq=seq)
    x4, x4s, lg = _matmul_ln_router(o, attn_w_out[0].astype(BF16), jnp.zeros((1, d), F32), x3,
                                    row(ln1_g[1]), row(ln1_b[1]), rw_hi, rw_lo)
    x5, x5b = _moe_layer(x4, x4s, lg, router_b, moe_w_in, moe_w_down, 1, row(ln2_g[1]), row(ln2_b[1]))
    x6, _ = _per_layer_input(x5b, x5, pb[1], ple_w_gate[1].astype(BF16), row(ple_b_gate[1]),
                             ple_w_proj[1].astype(BF16))
    return x6.reshape(batch, seq, d)
```

```python
import functools

import jax
import jax.numpy as jnp
from jax import lax
from jax.experimental import pallas as pl
from jax.experimental.pallas import tpu as pltpu

F32 = jnp.float32
BF16 = jnp.bfloat16

CONV_KERNEL = 31
N_HEADS = 32
QK_NOPE_DIM = 128
QK_ROPE_DIM = 64
V_HEAD_DIM = 128
KV_LORA_RANK = 512
ROPE_THETA = 10000.0
SOFTMAX_SCALE = (QK_NOPE_DIM + QK_ROPE_DIM) ** -0.5
N_EXPERTS = 32
N_GROUPS = 8
EXPERTS_PER_GROUP = N_EXPERTS // N_GROUPS
DEPTH = 2
ALPHA = (2 * DEPTH) ** 0.25
LN_EPS = 1e-5
RMS_EPS = 1e-6

LANES = 128
SUBLANES = 8
MOE_ROW_TILE = 256
WAIT_UNROLL = 8
VMEM_LIMIT = 56 * 1024 * 1024


def _cparams(*sem, vmem=VMEM_LIMIT):
    return pltpu.CompilerParams(dimension_semantics=sem, vmem_limit_bytes=vmem)


def _pick(n, pref):
    t = min(n, pref)
    while n % t:
        t //= 2
    return t


def _sigmoid(x):
    return 1.0 / (1.0 + jnp.exp(-x))


def _layer_norm(r, g, b):
    mu = jnp.mean(r, axis=-1, keepdims=True)
    rc = r - mu
    var = jnp.mean(rc * rc, axis=-1, keepdims=True)
    return rc * lax.rsqrt(var + LN_EPS) * g + b


def _mm_kernel(a_ref, w_ref, b_ref, o_ref, *, scale):
    acc = jnp.dot(a_ref[...], w_ref[...], preferred_element_type=F32)
    o_ref[...] = ((acc + b_ref[...]) * scale).astype(o_ref.dtype)


def _matmul(a, w, bias=None, *, scale=1.0, wide_out=False, tm=1024, tn=1024):
    m, kd = a.shape
    n = w.shape[1]
    tm, tn = _pick(m, tm), _pick(n, tn)
    if bias is None:
        bias = jnp.zeros((1, n), F32)
    return pl.pallas_call(
        functools.partial(_mm_kernel, scale=scale),
        grid=(m // tm, n // tn),
        in_specs=[
            pl.BlockSpec((tm, kd), lambda i, j: (i, 0)),
            pl.BlockSpec((kd, tn), lambda i, j: (0, j)),
            pl.BlockSpec((1, tn), lambda i, j: (0, j)),
        ],
        out_specs=pl.BlockSpec((tm, tn), lambda i, j: (i, j)),
        out_shape=jax.ShapeDtypeStruct((m, n), F32 if wide_out else BF16),
        compiler_params=_cparams("parallel", "parallel"),
        name="matmul",
    )(a, w, bias)


def _glu_kernel(a_ref, wa_ref, wg_ref, ba_ref, bg_ref, o_ref):
    a = a_ref[...]
    val = jnp.dot(a, wa_ref[...], preferred_element_type=F32) + ba_ref[...]
    gate = jnp.dot(a, wg_ref[...], preferred_element_type=F32) + bg_ref[...]
    o_ref[...] = (val * _sigmoid(gate)).astype(o_ref.dtype)


def _glu_matmul(a, w, bias, *, tm=1024, tn=512):
    m, kd = a.shape
    n = w.shape[1] // 2
    tm, tn = _pick(m, tm), _pick(n, tn)
    nj = n // tn
    return pl.pallas_call(
        _glu_kernel,
        grid=(m // tm, nj),
        in_specs=[
            pl.BlockSpec((tm, kd), lambda i, j: (i, 0)),
            pl.BlockSpec((kd, tn), lambda i, j: (0, j)),
            pl.BlockSpec((kd, tn), lambda i, j: (0, j + nj)),
            pl.BlockSpec((1, tn), lambda i, j: (0, j)),
            pl.BlockSpec((1, tn), lambda i, j: (0, j + nj)),
        ],
        out_specs=pl.BlockSpec((tm, tn), lambda i, j: (i, j)),
        out_shape=jax.ShapeDtypeStruct((m, n), F32),
        compiler_params=_cparams("parallel", "parallel"),
        name="glu_matmul",
    )(a, w, w, bias, bias)


CONV_HALO = 32
CONV_ROWS = 64
CONV_LEAD = CONV_HALO - (CONV_KERNEL - 1)


def _conv_ln_kernel(prev_ref, cur_ref, w_ref, bdw_ref, g_ref, b_ref, o_ref, win_ref, conv_ref, *, ts, d):
    i = pl.program_id(1)

    @pl.when(i == 0)
    def _():
        win_ref[0:CONV_HALO, :] = jnp.zeros((CONV_HALO, d), F32)

    @pl.when(i > 0)
    def _():
        win_ref[0:CONV_HALO, :] = prev_ref[...]

    win_ref[CONV_HALO:CONV_HALO + ts, :] = cur_ref[...]
    win_ref[CONV_HALO + ts:, :] = jnp.zeros((SUBLANES, d), F32)

    def cols_body(cb, carry):
        cols = pl.ds(pl.multiple_of(cb * LANES, LANES), LANES)
        for r0 in range(0, ts, CONV_ROWS):
            acc = None
            for c in range(SUBLANES):
                u = None
                for a in range((CONV_LEAD + CONV_KERNEL - 1) // SUBLANES + 1):
                    k = SUBLANES * a + c - CONV_LEAD
                    if 0 <= k < CONV_KERNEL:
                        term = win_ref[pl.ds(r0 + SUBLANES * a, CONV_ROWS + SUBLANES), cols] * w_ref[k:k + 1, cols]
                        u = term if u is None else u + term
                part = u[c:c + CONV_ROWS]
                acc = part if acc is None else acc + part
            conv_ref[pl.ds(r0, CONV_ROWS), cols] = acc
        return carry

    lax.fori_loop(0, d // LANES, cols_body, 0)

    h = _layer_norm(conv_ref[...] + bdw_ref[...], g_ref[...], b_ref[...])
    o_ref[...] = (h * _sigmoid(h)).astype(o_ref.dtype)


def _conv_ln_swish(h, w_dw, b_dw, g, b, *, batch, seq, ts=256):
    t, d = h.shape
    ts = _pick(seq, ts)
    nblk = seq // ts
    hb = ts // CONV_HALO
    return pl.pallas_call(
        functools.partial(_conv_ln_kernel, ts=ts, d=d),
        grid=(batch, nblk),
        in_specs=[
            pl.BlockSpec((CONV_HALO, d), lambda bi, i: (jnp.maximum((bi * nblk + i) * hb - 1, 0), 0)),
            pl.BlockSpec((ts, d), lambda bi, i: (bi * nblk + i, 0)),
            pl.BlockSpec((CONV_KERNEL, d), lambda bi, i: (0, 0)),
            pl.BlockSpec((1, d), lambda bi, i: (0, 0)),
            pl.BlockSpec((1, d), lambda bi, i: (0, 0)),
            pl.BlockSpec((1, d), lambda bi, i: (0, 0)),
        ],
        out_specs=pl.BlockSpec((ts, d), lambda bi, i: (bi * nblk + i, 0)),
        out_shape=jax.ShapeDtypeStruct((t, d), BF16),
        scratch_shapes=[pltpu.VMEM((CONV_HALO + ts + SUBLANES, d), F32), pltpu.VMEM((ts, d), F32)],
        compiler_params=_cparams("parallel", "arbitrary"),
        name="conv_ln_swish",
    )(h, h, w_dw, b_dw, g, b)


def _slab_pitch(nslab):
    pitch = -(-nslab // SUBLANES) * SUBLANES
    return pitch if (pitch // SUBLANES) % 2 else pitch + SUBLANES


def _store_slabs(slab_ref, y, first_slab, tm, pitch):
    for j in range(y.shape[1] // LANES):
        slab_ref[pl.ds(first_slab + j, tm, stride=pitch), :] = y[:, j * LANES:(j + 1) * LANES]


def _zero_pad_slabs(slab_ref, nslab, tm, pitch):
    for j in range(nslab, pitch):
        slab_ref[pl.ds(j, tm, stride=pitch), :] = jnp.zeros((tm, LANES), F32)


def _load_slabs(slab_ref, base, first_slab, count, tm, pitch):
    parts = [slab_ref[pl.ds(base + first_slab + j, tm, stride=pitch), :] for j in range(count)]
    return parts[0] if count == 1 else jnp.concatenate(parts, axis=1)


def _ln_router_kernel(mix_ref, res_ref, g_ref, b_ref, rwh_ref, rwl_ref, x_ref, slab_ref, lg_ref, *, tm, pitch):
    y = _layer_norm(ALPHA * res_ref[...] + mix_ref[...], g_ref[...], b_ref[...])
    x_ref[...] = y
    _store_slabs(slab_ref, y, 0, tm, pitch)
    _zero_pad_slabs(slab_ref, y.shape[1] // LANES, tm, pitch)
    yh = y.astype(BF16)
    yl = (y - yh.astype(F32)).astype(BF16)
    nt = (((1,), (1,)), ((), ()))
    lg = lax.dot_general(rwh_ref[...], yh, nt, preferred_element_type=F32)
    lg = lg + lax.dot_general(rwh_ref[...], yl, nt, preferred_element_type=F32)
    lg = lg + lax.dot_general(rwl_ref[...], yh, nt, preferred_element_type=F32)
    lg_ref[...] = lg


def _ln_router(mix, res, g, b, rw_hi, rw_lo, *, tm=256):
    m, d = mix.shape
    tm = _pick(m, tm)
    ne = rw_hi.shape[0]
    pitch = _slab_pitch(d // LANES)
    row = lambda i: (i, 0)
    const = lambda i: (0, 0)
    return pl.pallas_call(
        functools.partial(_ln_router_kernel, tm=tm, pitch=pitch),
        grid=(m // tm,),
        in_specs=[
            pl.BlockSpec((tm, d), row),
            pl.BlockSpec((tm, d), row),
            pl.BlockSpec((1, d), const),
            pl.BlockSpec((1, d), const),
            pl.BlockSpec((ne, d), const),
            pl.BlockSpec((ne, d), const),
        ],
        out_specs=[
            pl.BlockSpec((tm, d), row),
            pl.BlockSpec((tm * pitch, LANES), row),
            pl.BlockSpec((ne, tm), lambda i: (0, i)),
        ],
        out_shape=[
            jax.ShapeDtypeStruct((m, d), F32),
            jax.ShapeDtypeStruct((m * pitch, LANES), F32),
            jax.ShapeDtypeStruct((ne, m), F32),
        ],
        compiler_params=_cparams("parallel"),
        name="ln_router",
    )(mix, res, g, b, rw_hi, rw_lo)


def _route_kernel(lg_ref, rb_ref, e_ref, w_ref, rank_ref, cnt_ref, carry_ref, *, tt, nsteps):
    step = pl.program_id(0)

    @pl.when(step == 0)
    def _():
        carry_ref[...] = jnp.zeros_like(carry_ref)

    aff = _sigmoid(lg_ref[...])
    biased = aff + rb_ref[...]
    rows_b = [biased[e:e + 1, :] for e in range(N_EXPERTS)]
    rows_a = [aff[e:e + 1, :] for e in range(N_EXPERTS)]

    def top2_sum(v):
        m01, n01 = jnp.maximum(v[0], v[1]), jnp.minimum(v[0], v[1])
        m23, n23 = jnp.maximum(v[2], v[3]), jnp.minimum(v[2], v[3])
        return jnp.maximum(m01, m23) + jnp.maximum(jnp.minimum(m01, m23), jnp.maximum(n01, n23))

    best = top2_sum(rows_b[0:EXPERTS_PER_GROUP])
    gidx = jnp.zeros((1, tt), jnp.int32)
    for g in range(1, N_GROUPS):
        s = top2_sum(rows_b[g * EXPERTS_PER_GROUP:(g + 1) * EXPERTS_PER_GROUP])
        better = s > best
        gidx = jnp.where(better, g, gidx)
        best = jnp.where(better, s, best)

    vb = [rows_b[i] for i in range(EXPERTS_PER_GROUP)]
    va = [rows_a[i] for i in range(EXPERTS_PER_GROUP)]
    for g in range(1, N_GROUPS):
        sel = gidx == g
        for i in range(EXPERTS_PER_GROUP):
            vb[i] = jnp.where(sel, rows_b[g * EXPERTS_PER_GROUP + i], vb[i])
            va[i] = jnp.where(sel, rows_a[g * EXPERTS_PER_GROUP + i], va[i])

    i1 = jnp.zeros((1, tt), jnp.int32)
    b1, a1 = vb[0], va[0]
    for i in range(1, EXPERTS_PER_GROUP):
        better = vb[i] > b1
        i1 = jnp.where(better, i, i1)
        b1 = jnp.where(better, vb[i], b1)
        a1 = jnp.where(better, va[i], a1)
    i2 = jnp.where(i1 == 0, 1, 0).astype(jnp.int32)
    b2 = jnp.where(i1 == 0, vb[1], vb[0])
    a2 = jnp.where(i1 == 0, va[1], va[0])
    for i in range(1, EXPERTS_PER_GROUP):
        better = jnp.logical_and(vb[i] > b2, i1 != i)
        i2 = jnp.where(better, i, i2)
        b2 = jnp.where(better, vb[i], b2)
        a2 = jnp.where(better, va[i], a2)

    e1 = gidx * EXPERTS_PER_GROUP + i1
    e2 = gidx * EXPERTS_PER_GROUP + i2
    den = a1 + a2
    e_ref[0:1, :] = e1
    e_ref[1:2, :] = e2
    w_ref[0:1, :] = a1 / den
    w_ref[1:2, :] = a2 / den

    eid = lax.broadcasted_iota(jnp.int32, (N_EXPERTS, tt), 0)
    hit1 = eid == e1
    hit2 = eid == e2
    onehot = jnp.where(jnp.logical_or(hit1, hit2), 1.0, 0.0)
    src = lax.broadcasted_iota(jnp.int32, (tt, tt), 0)
    dst = lax.broadcasted_iota(jnp.int32, (tt, tt), 1)
    upper = jnp.where(src < dst, 1.0, 0.0).astype(BF16)
    before = jnp.dot(onehot.astype(BF16), upper, preferred_element_type=F32) + carry_ref[:, 0:1]
    rank_ref[0:1, :] = jnp.sum(jnp.where(hit1, before, 0.0), axis=0, keepdims=True).astype(jnp.int32)
    rank_ref[1:2, :] = jnp.sum(jnp.where(hit2, before, 0.0), axis=0, keepdims=True).astype(jnp.int32)
    carry_ref[...] += jnp.sum(onehot, axis=1, keepdims=True)

    @pl.when(step == nsteps - 1)
    def _():
        cnt_ref[...] = carry_ref[...]


def _route(logits_t, router_b, *, tt=512):
    ne, t = logits_t.shape
    tt = _pick(t, tt)
    nsteps = t // tt
    return pl.pallas_call(
        functools.partial(_route_kernel, tt=tt, nsteps=nsteps),
        grid=(nsteps,),
        in_specs=[
            pl.BlockSpec((ne, tt), lambda s: (0, s)),
            pl.BlockSpec((ne, 1), lambda s: (0, 0)),
        ],
        out_specs=[
            pl.BlockSpec((2, tt), lambda s: (0, s)),
            pl.BlockSpec((2, tt), lambda s: (0, s)),
            pl.BlockSpec((2, tt), lambda s: (0, s)),
            pl.BlockSpec((ne, LANES), lambda s: (0, 0)),
        ],
        out_shape=[
            jax.ShapeDtypeStruct((2, t), jnp.int32),
            jax.ShapeDtypeStruct((2, t), F32),
            jax.ShapeDtypeStruct((2, t), jnp.int32),
            jax.ShapeDtypeStruct((ne, LANES), F32),
        ],
        scratch_shapes=[pltpu.VMEM((ne, LANES), F32)],
        compiler_params=_cparams("arbitrary"),
        name="route",
    )(logits_t, router_b.reshape(ne, 1).astype(F32))


def _token_copy(src_ref, dst_ref, sem, src_tok, dst_tok, nslab, pitch):
    src = src_ref.at[pl.ds(pl.multiple_of(src_tok * pitch, SUBLANES), nslab)]
    dst = dst_ref.at[pl.ds(pl.multiple_of(dst_tok * pitch, SUBLANES), nslab)]
    return pltpu.make_async_copy(src, dst, sem)


def _ffn_up_kernel(te_ref, tv_ref, cur_ref, nxt_ref, slab_ref, wg_ref, wu_ref, o_ref,
                   buf_ref, sem, accg_ref, accu_ref, *, nk, tm, n_tiles, nslab, pitch):
    i = pl.program_id(0)
    k = pl.program_id(1)
    n_used = tv_ref[n_tiles]
    used = i < n_used
    slot = lax.rem(i, 2)
    spk = nslab // nk
    per_step = tm // nk

    def start(idx_ref, dst_slot, row):
        _token_copy(slab_ref, buf_ref, sem.at[dst_slot], idx_ref[0, 0, row], dst_slot * tm + row,
                    nslab, pitch).start()

    @pl.when(jnp.logical_and(used, jnp.logical_and(i == 0, k == 0)))
    def _():
        def body(r, carry):
            start(cur_ref, 0, r)
            return carry

        lax.fori_loop(0, tm, body, 0)

    @pl.when(jnp.logical_and(used, k == 0))
    def _():
        def body(r, carry):
            _token_copy(slab_ref, buf_ref, sem.at[slot], 0, slot * tm + r, nslab, pitch).wait()
            return carry

        lax.fori_loop(0, tm, body, 0, unroll=WAIT_UNROLL)
        accg_ref[...] = jnp.zeros_like(accg_ref)
        accu_ref[...] = jnp.zeros_like(accu_ref)

    def step():
        x = _load_slabs(buf_ref, slot * (tm * pitch), k * spk, spk, tm, pitch).astype(BF16)
        rows = pl.ds(pl.multiple_of(k * (spk * LANES), spk * LANES), spk * LANES)
        accg_ref[...] += jnp.dot(x, wg_ref[rows, :].astype(BF16), preferred_element_type=F32)
        accu_ref[...] += jnp.dot(x, wu_ref[rows, :].astype(BF16), preferred_element_type=F32)

    @pl.when(jnp.logical_and(used, i + 1 < n_used))
    def _():
        for r in range(per_step):
            start(nxt_ref, 1 - slot, k * per_step + r)
        step()

    @pl.when(jnp.logical_and(used, i + 1 >= n_used))
    def _():
        step()

    @pl.when(jnp.logical_and(used, k == nk - 1))
    def _():
        hg = accg_ref[...]
        o_ref[...] = (hg * _sigmoid(hg) * accu_ref[...]).astype(o_ref.dtype)

    @pl.when(jnp.logical_and(jnp.logical_not(used), k == nk - 1))
    def _():
        o_ref[...] = jnp.zeros_like(o_ref)


def _ffn_up(slabs, src_tok, w_in, layer, tile_expert, tile_valid, *, tm, tk=512):
    d = w_in.shape[2]
    de = w_in.shape[3] // 2
    nslab = d // LANES
    pitch = _slab_pitch(nslab)
    tk = _pick(d, tk)
    nk = d // tk
    n_tiles = src_tok.shape[0] // tm
    idx = src_tok.reshape(n_tiles, 1, tm)
    resident = dict(pipeline_mode=pl.Buffered(1))

    grid_spec = pltpu.PrefetchScalarGridSpec(
        num_scalar_prefetch=2,
        grid=(n_tiles, nk),
        in_specs=[
            pl.BlockSpec((1, 1, tm), lambda i, k, te, tv: (i, 0, 0), memory_space=pltpu.SMEM),
            pl.BlockSpec((1, 1, tm), lambda i, k, te, tv: (jnp.minimum(i + 1, n_tiles - 1), 0, 0),
                         memory_space=pltpu.SMEM),
            pl.BlockSpec(memory_space=pl.ANY),
            pl.BlockSpec((None, None, d, de), lambda i, k, te, tv: (layer, te[i], 0, 0), **resident),
            pl.BlockSpec((None, None, d, de), lambda i, k, te, tv: (layer, te[i], 0, 1), **resident),
        ],
        out_specs=pl.BlockSpec((tm, de), lambda i, k, te, tv: (i, 0)),
        scratch_shapes=[
            pltpu.VMEM((2 * tm * pitch, LANES), F32),
            pltpu.SemaphoreType.DMA((2,)),
            pltpu.VMEM((tm, de), F32),
            pltpu.VMEM((tm, de), F32),
        ],
    )
    return pl.pallas_call(
        functools.partial(_ffn_up_kernel, nk=nk, tm=tm, n_tiles=n_tiles, nslab=nslab, pitch=pitch),
        grid_spec=grid_spec,
        out_shape=jax.ShapeDtypeStruct((n_tiles * tm, de), BF16),
        compiler_params=_cparams("arbitrary", "arbitrary"),
        name="ffn_up",
    )(tile_expert, tile_valid, idx, idx, slabs, w_in, w_in)


def _ffn_down_kernel(te_ref, tv_ref, h_ref, w_ref, o_ref, *, tm, nslab, pitch, spj):
    i = pl.program_id(0)
    j = pl.program_id(1)
    valid = tv_ref[i] > 0

    @pl.when(valid)
    def _():
        cols = pl.ds(pl.multiple_of(j * (spj * LANES), spj * LANES), spj * LANES)
        y = jnp.dot(h_ref[...], w_ref[:, cols].astype(BF16), preferred_element_type=F32)
        _store_slabs(o_ref, y, j * spj, tm, pitch)

    @pl.when(jnp.logical_and(valid, j == 0))
    def _():
        _zero_pad_slabs(o_ref, nslab, tm, pitch)

    @pl.when(jnp.logical_and(jnp.logical_not(valid), j == 0))
    def _():
        o_ref[...] = jnp.zeros_like(o_ref)


def _ffn_down(h, w_down, layer, tile_expert, tile_valid, *, tm, tn=1024):
    r, de = h.shape
    d = w_down.shape[3]
    nslab = d // LANES
    pitch = _slab_pitch(nslab)
    tn = _pick(d, tn)
    nj = d // tn
    n_tiles = r // tm

    def ii(i, tv):
        return jnp.maximum(jnp.minimum(i, tv[n_tiles] - 1), 0)

    grid_spec = pltpu.PrefetchScalarGridSpec(
        num_scalar_prefetch=2,
        grid=(n_tiles, nj),
        in_specs=[
            pl.BlockSpec((tm, de), lambda i, j, te, tv: (ii(i, tv), 0)),
            pl.BlockSpec((None, None, de, d), lambda i, j, te, tv: (layer, te[i], 0, 0),
                         pipeline_mode=pl.Buffered(1)),
        ],
        out_specs=pl.BlockSpec((tm * pitch, LANES), lambda i, j, te, tv: (i, 0)),
    )
    return pl.pallas_call(
        functools.partial(_ffn_down_kernel, tm=tm, nslab=nslab, pitch=pitch, spj=tn // LANES),
        grid_spec=grid_spec,
        out_shape=jax.ShapeDtypeStruct((r * pitch, LANES), F32),
        compiler_params=_cparams("arbitrary", "arbitrary"),
        name="ffn_down",
    )(tile_expert, tile_valid, h, w_down)


def _combine_ln_kernel(cur_ref, nxt_ref, ys_ref, gw_ref, x_ref, g_ref, b_ref, xo_ref, xb_ref, buf_ref, sem,
                       *, tm, nb, nslab, pitch):
    i = pl.program_id(0)
    slot = lax.rem(i, 2)

    def copies(idx_ref, dst_slot, r):
        first = _token_copy(ys_ref, buf_ref, sem.at[dst_slot], idx_ref[0, 0, r],
                            (2 * dst_slot) * tm + r, nslab, pitch)
        second = _token_copy(ys_ref, buf_ref, sem.at[dst_slot], idx_ref[0, 0, tm + r],
                             (2 * dst_slot + 1) * tm + r, nslab, pitch)
        return first, second

    def issue(idx_ref, dst_slot):
        def body(r, carry):
            first, second = copies(idx_ref, dst_slot, r)
            first.start()
            second.start()
            return carry

        lax.fori_loop(0, tm, body, 0)

    @pl.when(i == 0)
    def _():
        issue(cur_ref, 0)

    @pl.when(i + 1 < nb)
    def _():
        issue(nxt_ref, 1 - slot)

    def drain(r, carry):
        first, second = copies(cur_ref, slot, r)
        first.wait()
        second.wait()
        return carry

    lax.fori_loop(0, tm, drain, 0, unroll=WAIT_UNROLL)
    gw = gw_ref[...]
    y0 = _load_slabs(buf_ref, (2 * slot) * (tm * pitch), 0, nslab, tm, pitch)
    y1 = _load_slabs(buf_ref, (2 * slot + 1) * (tm * pitch), 0, nslab, tm, pitch)
    y = gw[:, 0:1] * y0 + gw[:, 1:2] * y1
    out = _layer_norm(ALPHA * x_ref[...] + y, g_ref[...], b_ref[...])
    xo_ref[...] = out
    xb_ref[...] = out.astype(BF16)


def _combine_ln(ys, pos, gate_w, x, g, b, *, tm=256):
    t, d = x.shape
    tm = _pick(t, tm)
    nb = t // tm
    nslab = d // LANES
    pitch = _slab_pitch(nslab)
    pos_tiles = pos.reshape(2, nb, tm).transpose(1, 0, 2).reshape(nb, 1, 2 * tm)
    row = lambda i: (i, 0)
    const = lambda i: (0, 0)
    return pl.pallas_call(
        functools.partial(_combine_ln_kernel, tm=tm, nb=nb, nslab=nslab, pitch=pitch),
        grid=(nb,),
        in_specs=[
            pl.BlockSpec((1, 1, 2 * tm), lambda i: (i, 0, 0), memory_space=pltpu.SMEM),
            pl.BlockSpec((1, 1, 2 * tm), lambda i: (jnp.minimum(i + 1, nb - 1), 0, 0), memory_space=pltpu.SMEM),
            pl.BlockSpec(memory_space=pl.ANY),
            pl.BlockSpec((tm, 2), row),
            pl.BlockSpec((tm, d), row),
            pl.BlockSpec((1, d), const),
            pl.BlockSpec((1, d), const),
        ],
        out_specs=[pl.BlockSpec((tm, d), row), pl.BlockSpec((tm, d), row)],
        out_shape=[jax.ShapeDtypeStruct((t, d), F32), jax.ShapeDtypeStruct((t, d), BF16)],
        scratch_shapes=[pltpu.VMEM((4 * tm * pitch, LANES), F32), pltpu.SemaphoreType.DMA((2,))],
        compiler_params=_cparams("arbitrary"),
        name="combine_ln",
    )(pos_tiles, pos_tiles, ys, gate_w, x, g, b)


def _ple_kernel(a_ref, wg_ref, bg_ref, p_ref, wp_ref, x_ref, xo_ref, xb_ref):
    gate = _sigmoid(jnp.dot(a_ref[...], wg_ref[...], preferred_element_type=F32) + bg_ref[...])
    proj = jnp.dot(p_ref[...], wp_ref[...], preferred_element_type=F32)
    out = x_ref[...] + gate * proj
    xo_ref[...] = out
    xb_ref[...] = out.astype(BF16)


def _per_layer_input(xb, x, p, w_gate, b_gate, w_proj, *, tm=1024, tn=512):
    m, d = x.shape
    pd = p.shape[1]
    tm, tn = _pick(m, tm), _pick(d, tn)
    tile = lambda i, j: (i, j)
    return pl.pallas_call(
        _ple_kernel,
        grid=(m // tm, d // tn),
        in_specs=[
            pl.BlockSpec((tm, d), lambda i, j: (i, 0)),
            pl.BlockSpec((d, tn), lambda i, j: (0, j)),
            pl.BlockSpec((1, tn), lambda i, j: (0, j)),
            pl.BlockSpec((tm, pd), lambda i, j: (i, 0)),
            pl.BlockSpec((pd, tn), lambda i, j: (0, j)),
            pl.BlockSpec((tm, tn), tile),
        ],
        out_specs=[pl.BlockSpec((tm, tn), tile), pl.BlockSpec((tm, tn), tile)],
        out_shape=[jax.ShapeDtypeStruct((m, d), F32), jax.ShapeDtypeStruct((m, d), BF16)],
        compiler_params=_cparams("parallel", "parallel"),
        name="per_layer_input",
    )(xb, w_gate, b_gate, p, w_proj, x)


def _rope_table_kernel(pos_ref, cos_ref, sin_ref):
    half = QK_ROPE_DIM // 2
    lane = lax.broadcasted_iota(jnp.int32, (1, LANES), 1)
    freq = (lane % half).astype(F32)
    inv_freq = 1.0 / (ROPE_THETA ** (freq * (2.0 / QK_ROPE_DIM)))
    ang = pos_ref[...].astype(F32) * inv_freq
    sign = jnp.where((lane // half) % 2 == 0, -1.0, 1.0)
    cos_ref[...] = jnp.cos(ang)
    sin_ref[...] = jnp.sin(ang) * sign


def _rope_tables(positions, *, tm=2048):
    t = positions.size
    tm = _pick(t, tm)
    return pl.pallas_call(
        _rope_table_kernel,
        grid=(t // tm,),
        in_specs=[pl.BlockSpec((tm, 1), lambda i: (i, 0))],
        out_specs=[pl.BlockSpec((tm, LANES), lambda i: (i, 0))] * 2,
        out_shape=[jax.ShapeDtypeStruct((t, LANES), F32)] * 2,
        compiler_params=_cparams("parallel"),
        name="rope_tables",
    )(positions.reshape(t, 1))


def _latent_kernel(a_ref, w_ref, gq_ref, gkv_ref, cos_ref, sin_ref, cq_ref, ckv_ref, kr_ref, *, qr):
    c = jnp.dot(a_ref[...], w_ref[...], preferred_element_type=F32)
    cq = c[:, :qr]
    cq_ref[...] = (cq * lax.rsqrt(jnp.mean(cq * cq, axis=-1, keepdims=True) + RMS_EPS)
                   * gq_ref[...]).astype(BF16)
    ckv = c[:, qr:qr + KV_LORA_RANK]
    ckv_ref[...] = (ckv * lax.rsqrt(jnp.mean(ckv * ckv, axis=-1, keepdims=True) + RMS_EPS)
                    * gkv_ref[...]).astype(BF16)
    kpe = c[:, qr + KV_LORA_RANK:]
    cos = cos_ref[...]
    sin = sin_ref[...]
    rot = kpe[:, :QK_ROPE_DIM] * cos[:, :QK_ROPE_DIM] + kpe[:, QK_ROPE_DIM:] * sin[:, :QK_ROPE_DIM]
    kr_ref[...] = rot.astype(BF16)


def _latents(xb, w_cat, gq, gkv, cos_t, sin_t, *, qr, tm=512):
    m, kd = xb.shape
    n = w_cat.shape[1]
    tm = _pick(m, tm)
    row = lambda i: (i, 0)
    const = lambda i: (0, 0)
    return pl.pallas_call(
        functools.partial(_latent_kernel, qr=qr),
        grid=(m // tm,),
        in_specs=[
            pl.BlockSpec((tm, kd), row),
            pl.BlockSpec((kd, n), const),
            pl.BlockSpec((1, qr), const),
            pl.BlockSpec((1, KV_LORA_RANK), const),
            pl.BlockSpec((tm, LANES), row),
            pl.BlockSpec((tm, LANES), row),
        ],
        out_specs=[
            pl.BlockSpec((tm, qr), row),
            pl.BlockSpec((tm, KV_LORA_RANK), row),
            pl.BlockSpec((tm, QK_ROPE_DIM), row),
        ],
        out_shape=[
            jax.ShapeDtypeStruct((m, qr), BF16),
            jax.ShapeDtypeStruct((m, KV_LORA_RANK), BF16),
            jax.ShapeDtypeStruct((m, QK_ROPE_DIM), BF16),
        ],
        compiler_params=_cparams("parallel"),
        name="latents",
    )(xb, w_cat, gq, gkv, cos_t, sin_t)


def _rope_q_kernel(a_ref, w_ref, ws_ref, cos_ref, sin_ref, o_ref, *, scale, reps):
    a = a_ref[...]
    q = jnp.dot(a, w_ref[...], preferred_element_type=F32)
    qs = jnp.dot(a, ws_ref[...], preferred_element_type=F32)
    cos = jnp.tile(cos_ref[...], (1, reps))
    sin = jnp.tile(sin_ref[...], (1, reps))
    o_ref[...] = ((q * cos + qs * sin) * scale).astype(o_ref.dtype)


def _rope_q(cq, w_r, w_r_swapped, cos_t, sin_t, *, scale, tm=1024, tn=1024):
    m, kd = cq.shape
    n = w_r.shape[1]
    tm, tn = _pick(m, tm), _pick(n, tn)
    return pl.pallas_call(
        functools.partial(_rope_q_kernel, scale=scale, reps=tn // LANES),
        grid=(m // tm, n // tn),
        in_specs=[
            pl.BlockSpec((tm, kd), lambda i, j: (i, 0)),
            pl.BlockSpec((kd, tn), lambda i, j: (0, j)),
            pl.BlockSpec((kd, tn), lambda i, j: (0, j)),
            pl.BlockSpec((tm, LANES), lambda i, j: (i, 0)),
            pl.BlockSpec((tm, LANES), lambda i, j: (i, 0)),
        ],
        out_specs=pl.BlockSpec((tm, tn), lambda i, j: (i, j)),
        out_shape=jax.ShapeDtypeStruct((m, n), BF16),
        compiler_params=_cparams("parallel", "parallel"),
        name="rope_q",
    )(cq, w_r, w_r_swapped, cos_t, sin_t)


ATTN_BLOCK = 256


def _attn_kernel(qn_ref, qr_ref, kv_ref, kr_ref, o_ref, qc_ref, kc_ref, s_ref, p_ref, *, seq):
    blk = ATTN_BLOCK
    nt = (((1,), (1,)), ((), ()))
    qk = QK_NOPE_DIM + QK_ROPE_DIM
    hw = QK_NOPE_DIM + V_HEAD_DIM
    row = lax.broadcasted_iota(jnp.int32, (blk, blk), 0)
    col = lax.broadcasted_iota(jnp.int32, (blk, blk), 1)
    for h in range(2):
        qc_ref[:, :QK_NOPE_DIM] = qn_ref[:, h * QK_NOPE_DIM:(h + 1) * QK_NOPE_DIM]
        qc_ref[:, QK_NOPE_DIM:qk] = qr_ref[:, h * QK_ROPE_DIM:(h + 1) * QK_ROPE_DIM]
        kc_ref[:, :QK_NOPE_DIM] = kv_ref[:, h * hw:h * hw + QK_NOPE_DIM]
        kc_ref[:, QK_NOPE_DIM:qk] = kr_ref[...]
        for i in range(seq // blk):
            par = i % 2
            q = qc_ref[i * blk:(i + 1) * blk, :]
            m = None
            for j in range(i + 1):
                s = lax.dot_general(q, kc_ref[j * blk:(j + 1) * blk, :], nt, preferred_element_type=F32)
                if j == i:
                    s = jnp.where(col <= row, s, -jnp.inf)
                s_ref[par, :, j * blk:(j + 1) * blk] = s
                mj = jnp.max(s, axis=-1, keepdims=True)
                m = mj if m is None else jnp.maximum(m, mj)
            l = None
            for j in range(i + 1):
                p = jnp.exp(s_ref[par, :, j * blk:(j + 1) * blk] - m)
                lj = jnp.sum(p, axis=-1, keepdims=True)
                l = lj if l is None else l + lj
                p_ref[par, :, j * blk:(j + 1) * blk] = p.astype(BF16)
            w = (i + 1) * blk
            v = kv_ref[0:w, h * hw + QK_NOPE_DIM:(h + 1) * hw]
            acc = jnp.dot(p_ref[par, :, 0:w], v, preferred_element_type=F32)
            o_ref[i * blk:(i + 1) * blk, h * V_HEAD_DIM:(h + 1) * V_HEAD_DIM] = (acc / l).astype(o_ref.dtype)


def _attention(qn, qr, kv, kr, *, batch, seq):
    t = qn.shape[0]
    hp = N_HEADS // 2
    kvw = 2 * (QK_NOPE_DIM + V_HEAD_DIM)
    qk = QK_NOPE_DIM + QK_ROPE_DIM
    return pl.pallas_call(
        functools.partial(_attn_kernel, seq=seq),
        grid=(batch, hp),
        in_specs=[
            pl.BlockSpec((seq, 2 * QK_NOPE_DIM), lambda b, g: (b, g)),
            pl.BlockSpec((seq, 2 * QK_ROPE_DIM), lambda b, g: (b, g)),
            pl.BlockSpec((seq, kvw), lambda b, g: (b, g)),
            pl.BlockSpec((seq, QK_ROPE_DIM), lambda b, g: (b, 0)),
        ],
        out_specs=pl.BlockSpec((seq, 2 * V_HEAD_DIM), lambda b, g: (b, g)),
        out_shape=jax.ShapeDtypeStruct((t, N_HEADS * V_HEAD_DIM), BF16),
        scratch_shapes=[
            pltpu.VMEM((seq, qk), BF16),
            pltpu.VMEM((seq, qk), BF16),
            pltpu.VMEM((2, ATTN_BLOCK, seq), F32),
            pltpu.VMEM((2, ATTN_BLOCK, seq), BF16),
        ],
        compiler_params=_cparams("parallel", "parallel"),
        name="attention",
    )(qn, qr, kv, kr)


def _split_hi_lo(w):
    hi = w.astype(BF16)
    lo = (w - hi.astype(F32)).astype(BF16)
    return hi, lo


def _swap_halves(w, width):
    k, n = w.shape
    return w.reshape(k, n // width, 2, width // 2)[:, :, ::-1, :].reshape(k, n)


def _moe_layer(x, x_slabs, logits_t, router_b, w_in, w_down, layer, g, b):
    t, d = x.shape
    tm = MOE_ROW_TILE
    experts, gate_w, rank, counts = _route(logits_t, router_b)

    cnt = counts[:, 0].astype(jnp.int32)
    tiles_per_expert = (cnt + tm - 1) // tm
    tile_end = jnp.cumsum(tiles_per_expert)
    row_off = (tile_end - tiles_per_expert) * tm
    n_tiles = (2 * t) // tm + N_EXPERTS
    n_rows = n_tiles * tm
    tile_id = jnp.arange(n_tiles, dtype=jnp.int32)
    n_used = tile_end[-1].astype(jnp.int32)
    tile_valid = jnp.concatenate([(tile_id < n_used).astype(jnp.int32), n_used[None]])
    last_tile = jnp.maximum(n_used - 1, 0)
    tile_expert = jnp.searchsorted(tile_end, jnp.minimum(tile_id, last_tile), side="right").astype(jnp.int32)
    tile_expert = jnp.minimum(tile_expert, N_EXPERTS - 1)
    onehot = experts[:, :, None] == jnp.arange(N_EXPERTS, dtype=jnp.int32)
    pos = jnp.sum(jnp.where(onehot, row_off, 0), axis=-1) + rank
    tok = jnp.broadcast_to(jnp.arange(t, dtype=jnp.int32)[None, :], (2, t))
    src_tok = jnp.zeros((n_rows,), jnp.int32).at[pos.reshape(-1)].set(tok.reshape(-1))

    hmid = _ffn_up(x_slabs, src_tok, w_in, layer, tile_expert, tile_valid, tm=tm)
    ys = _ffn_down(hmid, w_down, layer, tile_expert, tile_valid, tm=tm)
    return _combine_ln(ys, pos, gate_w.T, x, g, b)


def kernel(x, p, positions, conv_w_in, conv_b_in, conv_w_dw, conv_b_dw, conv_ln_g, conv_ln_b, conv_w_out, conv_b_out, kv_w_down, kv_norm_g, kv_w_up, q_w_down, q_norm_g, q_w_up, attn_w_out, router_w, router_b, moe_w_in, moe_w_down, ln1_g, ln1_b, ln2_g, ln2_b, ple_w_gate, ple_b_gate, ple_w_proj):
    batch, seq, d = x.shape
    t = batch * seq
    row = lambda v: v.reshape(1, -1).astype(F32)

    x0 = x.reshape(t, d)
    rw_hi, rw_lo = _split_hi_lo(router_w.T)
    pb = p.reshape(DEPTH, t, -1).astype(BF16)

    glu = _glu_matmul(x0.astype(BF16), conv_w_in[0].astype(BF16), row(conv_b_in[0]))
    hn = _conv_ln_swish(glu, conv_w_dw[0].reshape(CONV_KERNEL, d), row(conv_b_dw[0]),
                        row(conv_ln_g[0]), row(conv_ln_b[0]), batch=batch, seq=seq)
    mix = _matmul(hn, conv_w_out[0].astype(BF16), row(conv_b_out[0]), wide_out=True)
    x1, x1s, lg = _ln_router(mix, x0, row(ln1_g[0]), row(ln1_b[0]), rw_hi, rw_lo)
    x2, x2b = _moe_layer(x1, x1s, lg, router_b, moe_w_in, moe_w_down, 0, row(ln2_g[0]), row(ln2_b[0]))
    x3, x3b = _per_layer_input(x2b, x2, pb[0], ple_w_gate[0].astype(BF16), row(ple_b_gate[0]),
                               ple_w_proj[0].astype(BF16))

    cos_t, sin_t = _rope_tables(positions)
    qr_rank = q_w_down.shape[2]
    w_kpe = kv_w_down[:, KV_LORA_RANK:]
    w_cat = jnp.concatenate([q_w_down[0], kv_w_down[:, :KV_LORA_RANK], w_kpe,
                             _swap_halves(w_kpe, QK_ROPE_DIM)], axis=1).astype(BF16)
    cq, ckv, k_rot = _latents(x3b, w_cat, row(q_norm_g[0]), row(kv_norm_g), cos_t, sin_t, qr=qr_rank)
    kv = _matmul(ckv, kv_w_up.reshape(KV_LORA_RANK, -1).astype(BF16))

    w_qn = q_w_up[0][:, :, :QK_NOPE_DIM].reshape(qr_rank, -1).astype(BF16)
    w_qr = q_w_up[0][:, :, QK_NOPE_DIM:].reshape(qr_rank, -1)
    qn = _matmul(cq, w_qn, scale=SOFTMAX_SCALE)
    q_rot = _rope_q(cq, w_qr.astype(BF16), _swap_halves(w_qr, QK_ROPE_DIM).astype(BF16), cos_t, sin_t,
                    scale=SOFTMAX_SCALE)
    o = _attention(qn, q_rot, kv, k_rot, batch=batch, seq=seq)
    mix = _matmul(o, attn_w_out[0].astype(BF16), wide_out=True)
    x4, x4s, lg = _ln_router(mix, x3, row(ln1_g[1]), row(ln1_b[1]), rw_hi, rw_lo)
    x5, x5b = _moe_layer(x4, x4s, lg, router_b, moe_w_in, moe_w_down, 1, row(ln2_g[1]), row(ln2_b[1]))
    x6, _ = _per_layer_input(x5b, x5, pb[1], ple_w_gate[1].astype(BF16), row(ple_b_gate[1]),
                             ple_w_proj[1].astype(BF16))
    return x6.reshape(batch, seq, d)
```

```python
import functools

import jax
import jax.numpy as jnp
from jax import lax
from jax.experimental import pallas as pl
from jax.experimental.pallas import tpu as pltpu

F32 = jnp.float32
BF16 = jnp.bfloat16

CONV_KERNEL = 31
N_HEADS = 32
QK_NOPE_DIM = 128
QK_ROPE_DIM = 64
V_HEAD_DIM = 128
KV_LORA_RANK = 512
ROPE_THETA = 10000.0
SOFTMAX_SCALE = (QK_NOPE_DIM + QK_ROPE_DIM) ** -0.5
N_EXPERTS = 32
N_GROUPS = 8
EXPERTS_PER_GROUP = N_EXPERTS // N_GROUPS
DEPTH = 2
ALPHA = (2 * DEPTH) ** 0.25
LN_EPS = 1e-5
RMS_EPS = 1e-6

LANES = 128
SUBLANES = 8
MOE_ROW_TILE = 256
WAIT_UNROLL = 8
ISSUE_UNROLL = 4
VMEM_LIMIT = 56 * 1024 * 1024


def _cparams(*sem, vmem=VMEM_LIMIT):
    return pltpu.CompilerParams(dimension_semantics=sem, vmem_limit_bytes=vmem)


def _pick(n, pref):
    t = min(n, pref)
    while n % t:
        t //= 2
    return t


def _sigmoid(x):
    return 1.0 / (1.0 + jnp.exp(-x))


def _layer_norm(r, g, b):
    mu = jnp.mean(r, axis=-1, keepdims=True)
    rc = r - mu
    var = jnp.mean(rc * rc, axis=-1, keepdims=True)
    return rc * lax.rsqrt(var + LN_EPS) * g + b


def _mm_kernel(a_ref, w_ref, b_ref, o_ref, *, scale):
    acc = jnp.dot(a_ref[...], w_ref[...], preferred_element_type=F32)
    o_ref[...] = ((acc + b_ref[...]) * scale).astype(o_ref.dtype)


def _matmul(a, w, bias=None, *, scale=1.0, wide_out=False, tm=1024, tn=1024):
    m, kd = a.shape
    n = w.shape[1]
    tm, tn = _pick(m, tm), _pick(n, tn)
    if bias is None:
        bias = jnp.zeros((1, n), F32)
    return pl.pallas_call(
        functools.partial(_mm_kernel, scale=scale),
        grid=(m // tm, n // tn),
        in_specs=[
            pl.BlockSpec((tm, kd), lambda i, j: (i, 0)),
            pl.BlockSpec((kd, tn), lambda i, j: (0, j)),
            pl.BlockSpec((1, tn), lambda i, j: (0, j)),
        ],
        out_specs=pl.BlockSpec((tm, tn), lambda i, j: (i, j)),
        out_shape=jax.ShapeDtypeStruct((m, n), F32 if wide_out else BF16),
        compiler_params=_cparams("parallel", "parallel"),
        name="matmul",
    )(a, w, bias)


def _glu_kernel(a_ref, wa_ref, wg_ref, ba_ref, bg_ref, o_ref):
    a = a_ref[...]
    val = jnp.dot(a, wa_ref[...], preferred_element_type=F32) + ba_ref[...]
    gate = jnp.dot(a, wg_ref[...], preferred_element_type=F32) + bg_ref[...]
    o_ref[...] = (val * _sigmoid(gate)).astype(o_ref.dtype)


def _glu_matmul(a, w, bias, *, tm=1024, tn=512):
    m, kd = a.shape
    n = w.shape[1] // 2
    tm, tn = _pick(m, tm), _pick(n, tn)
    nj = n // tn
    return pl.pallas_call(
        _glu_kernel,
        grid=(m // tm, nj),
        in_specs=[
            pl.BlockSpec((tm, kd), lambda i, j: (i, 0)),
            pl.BlockSpec((kd, tn), lambda i, j: (0, j)),
            pl.BlockSpec((kd, tn), lambda i, j: (0, j + nj)),
            pl.BlockSpec((1, tn), lambda i, j: (0, j)),
            pl.BlockSpec((1, tn), lambda i, j: (0, j + nj)),
        ],
        out_specs=pl.BlockSpec((tm, tn), lambda i, j: (i, j)),
        out_shape=jax.ShapeDtypeStruct((m, n), F32),
        compiler_params=_cparams("parallel", "parallel"),
        name="glu_matmul",
    )(a, w, w, bias, bias)


CONV_HALO = 32
CONV_ROWS = 64
CONV_LEAD = CONV_HALO - (CONV_KERNEL - 1)


def _conv_ln_kernel(prev_ref, cur_ref, w_ref, bdw_ref, g_ref, b_ref, o_ref, win_ref, conv_ref, *, ts, d):
    i = pl.program_id(1)

    @pl.when(i == 0)
    def _():
        win_ref[0:CONV_HALO, :] = jnp.zeros((CONV_HALO, d), F32)

    @pl.when(i > 0)
    def _():
        win_ref[0:CONV_HALO, :] = prev_ref[...]

    win_ref[CONV_HALO:CONV_HALO + ts, :] = cur_ref[...]
    win_ref[CONV_HALO + ts:, :] = jnp.zeros((SUBLANES, d), F32)

    def cols_body(cb, carry):
        cols = pl.ds(pl.multiple_of(cb * LANES, LANES), LANES)
        for r0 in range(0, ts, CONV_ROWS):
            acc = None
            for c in range(SUBLANES):
                u = None
                for a in range((CONV_LEAD + CONV_KERNEL - 1) // SUBLANES + 1):
                    k = SUBLANES * a + c - CONV_LEAD
                    if 0 <= k < CONV_KERNEL:
                        term = win_ref[pl.ds(r0 + SUBLANES * a, CONV_ROWS + SUBLANES), cols] * w_ref[k:k + 1, cols]
                        u = term if u is None else u + term
                part = u[c:c + CONV_ROWS]
                acc = part if acc is None else acc + part
            conv_ref[pl.ds(r0, CONV_ROWS), cols] = acc
        return carry

    lax.fori_loop(0, d // LANES, cols_body, 0)

    h = _layer_norm(conv_ref[...] + bdw_ref[...], g_ref[...], b_ref[...])
    o_ref[...] = (h * _sigmoid(h)).astype(o_ref.dtype)


def _conv_ln_swish(h, w_dw, b_dw, g, b, *, batch, seq, ts=256):
    t, d = h.shape
    ts = _pick(seq, ts)
    nblk = seq // ts
    hb = ts // CONV_HALO
    return pl.pallas_call(
        functools.partial(_conv_ln_kernel, ts=ts, d=d),
        grid=(batch, nblk),
        in_specs=[
            pl.BlockSpec((CONV_HALO, d), lambda bi, i: (jnp.maximum((bi * nblk + i) * hb - 1, 0), 0)),
            pl.BlockSpec((ts, d), lambda bi, i: (bi * nblk + i, 0)),
            pl.BlockSpec((CONV_KERNEL, d), lambda bi, i: (0, 0)),
            pl.BlockSpec((1, d), lambda bi, i: (0, 0)),
            pl.BlockSpec((1, d), lambda bi, i: (0, 0)),
            pl.BlockSpec((1, d), lambda bi, i: (0, 0)),
        ],
        out_specs=pl.BlockSpec((ts, d), lambda bi, i: (bi * nblk + i, 0)),
        out_shape=jax.ShapeDtypeStruct((t, d), BF16),
        scratch_shapes=[pltpu.VMEM((CONV_HALO + ts + SUBLANES, d), F32), pltpu.VMEM((ts, d), F32)],
        compiler_params=_cparams("parallel", "arbitrary"),
        name="conv_ln_swish",
    )(h, h, w_dw, b_dw, g, b)


def _slab_pitch(nslab):
    pitch = -(-nslab // SUBLANES) * SUBLANES
    return pitch if (pitch // SUBLANES) % 2 else pitch + SUBLANES


def _store_slabs(slab_ref, y, first_slab, tm, pitch):
    for j in range(y.shape[1] // LANES):
        slab_ref[pl.ds(first_slab + j, tm, stride=pitch), :] = y[:, j * LANES:(j + 1) * LANES]


def _zero_pad_slabs(slab_ref, nslab, tm, pitch):
    for j in range(nslab, pitch):
        slab_ref[pl.ds(j, tm, stride=pitch), :] = jnp.zeros((tm, LANES), F32)


def _load_slabs(slab_ref, base, first_slab, count, tm, pitch):
    parts = [slab_ref[pl.ds(base + first_slab + j, tm, stride=pitch), :] for j in range(count)]
    return parts[0] if count == 1 else jnp.concatenate(parts, axis=1)


def _ln_router_kernel(mix_ref, res_ref, g_ref, b_ref, rwh_ref, rwl_ref, x_ref, slab_ref, lg_ref, *, tm, pitch):
    y = _layer_norm(ALPHA * res_ref[...] + mix_ref[...], g_ref[...], b_ref[...])
    x_ref[...] = y
    _store_slabs(slab_ref, y, 0, tm, pitch)
    _zero_pad_slabs(slab_ref, y.shape[1] // LANES, tm, pitch)
    yh = y.astype(BF16)
    yl = (y - yh.astype(F32)).astype(BF16)
    nt = (((1,), (1,)), ((), ()))
    lg = lax.dot_general(rwh_ref[...], yh, nt, preferred_element_type=F32)
    lg = lg + lax.dot_general(rwh_ref[...], yl, nt, preferred_element_type=F32)
    lg = lg + lax.dot_general(rwl_ref[...], yh, nt, preferred_element_type=F32)
    lg_ref[...] = lg


def _ln_router(mix, res, g, b, rw_hi, rw_lo, *, tm=256):
    m, d = mix.shape
    tm = _pick(m, tm)
    ne = rw_hi.shape[0]
    pitch = _slab_pitch(d // LANES)
    row = lambda i: (i, 0)
    const = lambda i: (0, 0)
    return pl.pallas_call(
        functools.partial(_ln_router_kernel, tm=tm, pitch=pitch),
        grid=(m // tm,),
        in_specs=[
            pl.BlockSpec((tm, d), row),
            pl.BlockSpec((tm, d), row),
            pl.BlockSpec((1, d), const),
            pl.BlockSpec((1, d), const),
            pl.BlockSpec((ne, d), const),
            pl.BlockSpec((ne, d), const),
        ],
        out_specs=[
            pl.BlockSpec((tm, d), row),
            pl.BlockSpec((tm * pitch, LANES), row),
            pl.BlockSpec((ne, tm), lambda i: (0, i)),
        ],
        out_shape=[
            jax.ShapeDtypeStruct((m, d), F32),
            jax.ShapeDtypeStruct((m * pitch, LANES), F32),
            jax.ShapeDtypeStruct((ne, m), F32),
        ],
        compiler_params=_cparams("parallel"),
        name="ln_router",
    )(mix, res, g, b, rw_hi, rw_lo)


def _route_kernel(lg_ref, rb_ref, e_ref, w_ref, rank_ref, cnt_ref, carry_ref, *, tt, nsteps):
    step = pl.program_id(0)

    @pl.when(step == 0)
    def _():
        carry_ref[...] = jnp.zeros_like(carry_ref)

    aff = _sigmoid(lg_ref[...])
    biased = aff + rb_ref[...]
    rows_b = [biased[e:e + 1, :] for e in range(N_EXPERTS)]
    rows_a = [aff[e:e + 1, :] for e in range(N_EXPERTS)]

    def top2_sum(v):
        m01, n01 = jnp.maximum(v[0], v[1]), jnp.minimum(v[0], v[1])
        m23, n23 = jnp.maximum(v[2], v[3]), jnp.minimum(v[2], v[3])
        return jnp.maximum(m01, m23) + jnp.maximum(jnp.minimum(m01, m23), jnp.maximum(n01, n23))

    best = top2_sum(rows_b[0:EXPERTS_PER_GROUP])
    gidx = jnp.zeros((1, tt), jnp.int32)
    for g in range(1, N_GROUPS):
        s = top2_sum(rows_b[g * EXPERTS_PER_GROUP:(g + 1) * EXPERTS_PER_GROUP])
        better = s > best
        gidx = jnp.where(better, g, gidx)
        best = jnp.where(better, s, best)

    vb = [rows_b[i] for i in range(EXPERTS_PER_GROUP)]
    va = [rows_a[i] for i in range(EXPERTS_PER_GROUP)]
    for g in range(1, N_GROUPS):
        sel = gidx == g
        for i in range(EXPERTS_PER_GROUP):
            vb[i] = jnp.where(sel, rows_b[g * EXPERTS_PER_GROUP + i], vb[i])
            va[i] = jnp.where(sel, rows_a[g * EXPERTS_PER_GROUP + i], va[i])

    i1 = jnp.zeros((1, tt), jnp.int32)
    b1, a1 = vb[0], va[0]
    for i in range(1, EXPERTS_PER_GROUP):
        better = vb[i] > b1
        i1 = jnp.where(better, i, i1)
        b1 = jnp.where(better, vb[i], b1)
        a1 = jnp.where(better, va[i], a1)
    i2 = jnp.where(i1 == 0, 1, 0).astype(jnp.int32)
    b2 = jnp.where(i1 == 0, vb[1], vb[0])
    a2 = jnp.where(i1 == 0, va[1], va[0])
    for i in range(1, EXPERTS_PER_GROUP):
        better = jnp.logical_and(vb[i] > b2, i1 != i)
        i2 = jnp.where(better, i, i2)
        b2 = jnp.where(better, vb[i], b2)
        a2 = jnp.where(better, va[i], a2)

    e1 = gidx * EXPERTS_PER_GROUP + i1
    e2 = gidx * EXPERTS_PER_GROUP + i2
    den = a1 + a2
    e_ref[0:1, :] = e1
    e_ref[1:2, :] = e2
    w_ref[0:1, :] = a1 / den
    w_ref[1:2, :] = a2 / den

    eid = lax.broadcasted_iota(jnp.int32, (N_EXPERTS, tt), 0)
    hit1 = eid == e1
    hit2 = eid == e2
    onehot = jnp.where(jnp.logical_or(hit1, hit2), 1.0, 0.0)
    src = lax.broadcasted_iota(jnp.int32, (tt, tt), 0)
    dst = lax.broadcasted_iota(jnp.int32, (tt, tt), 1)
    upper = jnp.where(src < dst, 1.0, 0.0).astype(BF16)
    before = jnp.dot(onehot.astype(BF16), upper, preferred_element_type=F32) + carry_ref[:, 0:1]
    rank_ref[0:1, :] = jnp.sum(jnp.where(hit1, before, 0.0), axis=0, keepdims=True).astype(jnp.int32)
    rank_ref[1:2, :] = jnp.sum(jnp.where(hit2, before, 0.0), axis=0, keepdims=True).astype(jnp.int32)
    carry_ref[...] += jnp.sum(onehot, axis=1, keepdims=True)

    @pl.when(step == nsteps - 1)
    def _():
        cnt_ref[...] = carry_ref[...]


def _route(logits_t, router_b, *, tt=512):
    ne, t = logits_t.shape
    tt = _pick(t, tt)
    nsteps = t // tt
    return pl.pallas_call(
        functools.partial(_route_kernel, tt=tt, nsteps=nsteps),
        grid=(nsteps,),
        in_specs=[
            pl.BlockSpec((ne, tt), lambda s: (0, s)),
            pl.BlockSpec((ne, 1), lambda s: (0, 0)),
        ],
        out_specs=[
            pl.BlockSpec((2, tt), lambda s: (0, s)),
            pl.BlockSpec((2, tt), lambda s: (0, s)),
            pl.BlockSpec((2, tt), lambda s: (0, s)),
            pl.BlockSpec((ne, LANES), lambda s: (0, 0)),
        ],
        out_shape=[
            jax.ShapeDtypeStruct((2, t), jnp.int32),
            jax.ShapeDtypeStruct((2, t), F32),
            jax.ShapeDtypeStruct((2, t), jnp.int32),
            jax.ShapeDtypeStruct((ne, LANES), F32),
        ],
        scratch_shapes=[pltpu.VMEM((ne, LANES), F32)],
        compiler_params=_cparams("arbitrary"),
        name="route",
    )(logits_t, router_b.reshape(ne, 1).astype(F32))


def _token_copy(src_ref, dst_ref, sem, src_tok, dst_tok, nslab, pitch):
    src = src_ref.at[pl.ds(pl.multiple_of(src_tok * pitch, SUBLANES), nslab)]
    dst = dst_ref.at[pl.ds(pl.multiple_of(dst_tok * pitch, SUBLANES), nslab)]
    return pltpu.make_async_copy(src, dst, sem)


def _ffn_up_kernel(te_ref, tv_ref, cur_ref, nxt_ref, slab_ref, wg_ref, wu_ref, o_ref,
                   buf_ref, sem, *, nk, tm, n_tiles, nslab, pitch):
    i = pl.program_id(0)
    n_used = tv_ref[n_tiles]
    used = i < n_used
    slot = lax.rem(i, 2)
    spk = nslab // nk

    def issue(idx_ref, dst_slot):
        def body(r, carry):
            _token_copy(slab_ref, buf_ref, sem.at[dst_slot], idx_ref[0, 0, r], dst_slot * tm + r,
                        nslab, pitch).start()
            return carry

        lax.fori_loop(0, tm, body, 0, unroll=ISSUE_UNROLL)

    @pl.when(jnp.logical_and(used, i == 0))
    def _():
        issue(cur_ref, 0)

    @pl.when(i + 1 < n_used)
    def _():
        issue(nxt_ref, 1 - slot)

    @pl.when(used)
    def _():
        def drain(r, carry):
            _token_copy(slab_ref, buf_ref, sem.at[slot], 0, slot * tm + r, nslab, pitch).wait()
            return carry

        lax.fori_loop(0, tm, drain, 0, unroll=WAIT_UNROLL)
        hg = hu = None
        for k in range(nk):
            x = _load_slabs(buf_ref, slot * (tm * pitch), k * spk, spk, tm, pitch).astype(BF16)
            rows = slice(k * spk * LANES, (k + 1) * spk * LANES)
            pg = jnp.dot(x, wg_ref[rows, :].astype(BF16), preferred_element_type=F32)
            pu = jnp.dot(x, wu_ref[rows, :].astype(BF16), preferred_element_type=F32)
            hg = pg if hg is None else hg + pg
            hu = pu if hu is None else hu + pu
        o_ref[...] = (hg * _sigmoid(hg) * hu).astype(o_ref.dtype)

    @pl.when(jnp.logical_not(used))
    def _():
        o_ref[...] = jnp.zeros_like(o_ref)


def _ffn_up(slabs, src_tok, w_in, layer, tile_expert, tile_valid, *, tm, tk=1024):
    d = w_in.shape[2]
    de = w_in.shape[3] // 2
    nslab = d // LANES
    pitch = _slab_pitch(nslab)
    tk = _pick(d, tk)
    nk = d // tk
    n_tiles = src_tok.shape[0] // tm
    idx = src_tok.reshape(n_tiles, 1, tm)
    resident = dict(pipeline_mode=pl.Buffered(1))

    grid_spec = pltpu.PrefetchScalarGridSpec(
        num_scalar_prefetch=2,
        grid=(n_tiles,),
        in_specs=[
            pl.BlockSpec((1, 1, tm), lambda i, te, tv: (i, 0, 0), memory_space=pltpu.SMEM),
            pl.BlockSpec((1, 1, tm), lambda i, te, tv: (jnp.minimum(i + 1, n_tiles - 1), 0, 0),
                         memory_space=pltpu.SMEM),
            pl.BlockSpec(memory_space=pl.ANY),
            pl.BlockSpec((None, None, d, de), lambda i, te, tv: (layer, te[i], 0, 0), **resident),
            pl.BlockSpec((None, None, d, de), lambda i, te, tv: (layer, te[i], 0, 1), **resident),
        ],
        out_specs=pl.BlockSpec((tm, de), lambda i, te, tv: (i, 0)),
        scratch_shapes=[
            pltpu.VMEM((2 * tm * pitch, LANES), F32),
            pltpu.SemaphoreType.DMA((2,)),
        ],
    )
    return pl.pallas_call(
        functools.partial(_ffn_up_kernel, nk=nk, tm=tm, n_tiles=n_tiles, nslab=nslab, pitch=pitch),
        grid_spec=grid_spec,
        out_shape=jax.ShapeDtypeStruct((n_tiles * tm, de), BF16),
        compiler_params=_cparams("arbitrary"),
        name="ffn_up",
    )(tile_expert, tile_valid, idx, idx, slabs, w_in, w_in)


def _ffn_down_kernel(te_ref, tv_ref, h_ref, w_ref, o_ref, *, tm, nslab, pitch, spj):
    i = pl.program_id(0)
    valid = tv_ref[i] > 0

    @pl.when(valid)
    def _():
        h = h_ref[...]
        for j in range(nslab // spj):
            cols = slice(j * spj * LANES, (j + 1) * spj * LANES)
            y = jnp.dot(h, w_ref[:, cols].astype(BF16), preferred_element_type=F32)
            _store_slabs(o_ref, y, j * spj, tm, pitch)
        _zero_pad_slabs(o_ref, nslab, tm, pitch)

    @pl.when(jnp.logical_not(valid))
    def _():
        o_ref[...] = jnp.zeros_like(o_ref)


def _ffn_down(h, w_down, layer, tile_expert, tile_valid, *, tm, tn=1024):
    r, de = h.shape
    d = w_down.shape[3]
    nslab = d // LANES
    pitch = _slab_pitch(nslab)
    tn = _pick(d, tn)
    n_tiles = r // tm

    def ii(i, tv):
        return jnp.maximum(jnp.minimum(i, tv[n_tiles] - 1), 0)

    grid_spec = pltpu.PrefetchScalarGridSpec(
        num_scalar_prefetch=2,
        grid=(n_tiles,),
        in_specs=[
            pl.BlockSpec((tm, de), lambda i, te, tv: (ii(i, tv), 0)),
            pl.BlockSpec((None, None, de, d), lambda i, te, tv: (layer, te[i], 0, 0),
                         pipeline_mode=pl.Buffered(1)),
        ],
        out_specs=pl.BlockSpec((tm * pitch, LANES), lambda i, te, tv: (i, 0)),
    )
    return pl.pallas_call(
        functools.partial(_ffn_down_kernel, tm=tm, nslab=nslab, pitch=pitch, spj=tn // LANES),
        grid_spec=grid_spec,
        out_shape=jax.ShapeDtypeStruct((r * pitch, LANES), F32),
        compiler_params=_cparams("arbitrary"),
        name="ffn_down",
    )(tile_expert, tile_valid, h, w_down)


def _combine_ln_kernel(cur_ref, nxt_ref, ys_ref, gw_ref, x_ref, g_ref, b_ref, xo_ref, xb_ref, buf_ref, sem,
                       *, tm, nb, nslab, pitch):
    i = pl.program_id(0)
    slot = lax.rem(i, 2)

    def copies(idx_ref, dst_slot, r):
        first = _token_copy(ys_ref, buf_ref, sem.at[dst_slot], idx_ref[0, 0, r],
                            (2 * dst_slot) * tm + r, nslab, pitch)
        second = _token_copy(ys_ref, buf_ref, sem.at[dst_slot], idx_ref[0, 0, tm + r],
                             (2 * dst_slot + 1) * tm + r, nslab, pitch)
        return first, second

    def issue(idx_ref, dst_slot):
        def body(r, carry):
            first, second = copies(idx_ref, dst_slot, r)
            first.start()
            second.start()
            return carry

        lax.fori_loop(0, tm, body, 0, unroll=ISSUE_UNROLL)

    @pl.when(i == 0)
    def _():
        issue(cur_ref, 0)

    @pl.when(i + 1 < nb)
    def _():
        issue(nxt_ref, 1 - slot)

    def drain(r, carry):
        first, second = copies(cur_ref, slot, r)
        first.wait()
        second.wait()
        return carry

    lax.fori_loop(0, tm, drain, 0, unroll=WAIT_UNROLL)
    gw = gw_ref[...]
    y0 = _load_slabs(buf_ref, (2 * slot) * (tm * pitch), 0, nslab, tm, pitch)
    y1 = _load_slabs(buf_ref, (2 * slot + 1) * (tm * pitch), 0, nslab, tm, pitch)
    y = gw[:, 0:1] * y0 + gw[:, 1:2] * y1
    out = _layer_norm(ALPHA * x_ref[...] + y, g_ref[...], b_ref[...])
    xo_ref[...] = out
    xb_ref[...] = out.astype(BF16)


def _combine_ln(ys, pos, gate_w, x, g, b, *, tm=256):
    t, d = x.shape
    tm = _pick(t, tm)
    nb = t // tm
    nslab = d // LANES
    pitch = _slab_pitch(nslab)
    pos_tiles = pos.reshape(2, nb, tm).transpose(1, 0, 2).reshape(nb, 1, 2 * tm)
    row = lambda i: (i, 0)
    const = lambda i: (0, 0)
    return pl.pallas_call(
        functools.partial(_combine_ln_kernel, tm=tm, nb=nb, nslab=nslab, pitch=pitch),
        grid=(nb,),
        in_specs=[
            pl.BlockSpec((1, 1, 2 * tm), lambda i: (i, 0, 0), memory_space=pltpu.SMEM),
            pl.BlockSpec((1, 1, 2 * tm), lambda i: (jnp.minimum(i + 1, nb - 1), 0, 0), memory_space=pltpu.SMEM),
            pl.BlockSpec(memory_space=pl.ANY),
            pl.BlockSpec((tm, 2), row),
            pl.BlockSpec((tm, d), row),
            pl.BlockSpec((1, d), const),
            pl.BlockSpec((1, d), const),
        ],
        out_specs=[pl.BlockSpec((tm, d), row), pl.BlockSpec((tm, d), row)],
        out_shape=[jax.ShapeDtypeStruct((t, d), F32), jax.ShapeDtypeStruct((t, d), BF16)],
        scratch_shapes=[pltpu.VMEM((4 * tm * pitch, LANES), F32), pltpu.SemaphoreType.DMA((2,))],
        compiler_params=_cparams("arbitrary"),
        name="combine_ln",
    )(pos_tiles, pos_tiles, ys, gate_w, x, g, b)


def _ple_kernel(a_ref, wg_ref, bg_ref, p_ref, wp_ref, x_ref, xo_ref, xb_ref):
    gate = _sigmoid(jnp.dot(a_ref[...], wg_ref[...], preferred_element_type=F32) + bg_ref[...])
    proj = jnp.dot(p_ref[...], wp_ref[...], preferred_element_type=F32)
    out = x_ref[...] + gate * proj
    xo_ref[...] = out
    xb_ref[...] = out.astype(BF16)


def _per_layer_input(xb, x, p, w_gate, b_gate, w_proj, *, tm=1024, tn=512):
    m, d = x.shape
    pd = p.shape[1]
    tm, tn = _pick(m, tm), _pick(d, tn)
    tile = lambda i, j: (i, j)
    return pl.pallas_call(
        _ple_kernel,
        grid=(m // tm, d // tn),
        in_specs=[
            pl.BlockSpec((tm, d), lambda i, j: (i, 0)),
            pl.BlockSpec((d, tn), lambda i, j: (0, j)),
            pl.BlockSpec((1, tn), lambda i, j: (0, j)),
            pl.BlockSpec((tm, pd), lambda i, j: (i, 0)),
            pl.BlockSpec((pd, tn), lambda i, j: (0, j)),
            pl.BlockSpec((tm, tn), tile),
        ],
        out_specs=[pl.BlockSpec((tm, tn), tile), pl.BlockSpec((tm, tn), tile)],
        out_shape=[jax.ShapeDtypeStruct((m, d), F32), jax.ShapeDtypeStruct((m, d), BF16)],
        compiler_params=_cparams("parallel", "parallel"),
        name="per_layer_input",
    )(xb, w_gate, b_gate, p, w_proj, x)


def _rope_table_kernel(pos_ref, cos_ref, sin_ref):
    half = QK_ROPE_DIM // 2
    lane = lax.broadcasted_iota(jnp.int32, (1, LANES), 1)
    freq = (lane % half).astype(F32)
    inv_freq = 1.0 / (ROPE_THETA ** (freq * (2.0 / QK_ROPE_DIM)))
    ang = pos_ref[...].astype(F32) * inv_freq
    sign = jnp.where((lane // half) % 2 == 0, -1.0, 1.0)
    cos_ref[...] = jnp.cos(ang)
    sin_ref[...] = jnp.sin(ang) * sign


def _rope_tables(positions, *, tm=2048):
    t = positions.size
    tm = _pick(t, tm)
    return pl.pallas_call(
        _rope_table_kernel,
        grid=(t // tm,),
        in_specs=[pl.BlockSpec((tm, 1), lambda i: (i, 0))],
        out_specs=[pl.BlockSpec((tm, LANES), lambda i: (i, 0))] * 2,
        out_shape=[jax.ShapeDtypeStruct((t, LANES), F32)] * 2,
        compiler_params=_cparams("parallel"),
        name="rope_tables",
    )(positions.reshape(t, 1))


def _latent_kernel(a_ref, w_ref, gq_ref, gkv_ref, cos_ref, sin_ref, cq_ref, ckv_ref, kr_ref, *, qr):
    c = jnp.dot(a_ref[...], w_ref[...], preferred_element_type=F32)
    cq = c[:, :qr]
    cq_ref[...] = (cq * lax.rsqrt(jnp.mean(cq * cq, axis=-1, keepdims=True) + RMS_EPS)
                   * gq_ref[...]).astype(BF16)
    ckv = c[:, qr:qr + KV_LORA_RANK]
    ckv_ref[...] = (ckv * lax.rsqrt(jnp.mean(ckv * ckv, axis=-1, keepdims=True) + RMS_EPS)
                    * gkv_ref[...]).astype(BF16)
    kpe = c[:, qr + KV_LORA_RANK:]
    cos = cos_ref[...]
    sin = sin_ref[...]
    rot = kpe[:, :QK_ROPE_DIM] * cos[:, :QK_ROPE_DIM] + kpe[:, QK_ROPE_DIM:] * sin[:, :QK_ROPE_DIM]
    kr_ref[...] = rot.astype(BF16)


def _latents(xb, w_cat, gq, gkv, cos_t, sin_t, *, qr, tm=512):
    m, kd = xb.shape
    n = w_cat.shape[1]
    tm = _pick(m, tm)
    row = lambda i: (i, 0)
    const = lambda i: (0, 0)
    return pl.pallas_call(
        functools.partial(_latent_kernel, qr=qr),
        grid=(m // tm,),
        in_specs=[
            pl.BlockSpec((tm, kd), row),
            pl.BlockSpec((kd, n), const),
            pl.BlockSpec((1, qr), const),
            pl.BlockSpec((1, KV_LORA_RANK), const),
            pl.BlockSpec((tm, LANES), row),
            pl.BlockSpec((tm, LANES), row),
        ],
        out_specs=[
            pl.BlockSpec((tm, qr), row),
            pl.BlockSpec((tm, KV_LORA_RANK), row),
            pl.BlockSpec((tm, QK_ROPE_DIM), row),
        ],
        out_shape=[
            jax.ShapeDtypeStruct((m, qr), BF16),
            jax.ShapeDtypeStruct((m, KV_LORA_RANK), BF16),
            jax.ShapeDtypeStruct((m, QK_ROPE_DIM), BF16),
        ],
        compiler_params=_cparams("parallel"),
        name="latents",
    )(xb, w_cat, gq, gkv, cos_t, sin_t)


def _rope_q_kernel(a_ref, w_ref, ws_ref, cos_ref, sin_ref, o_ref, *, scale, reps):
    a = a_ref[...]
    q = jnp.dot(a, w_ref[...], preferred_element_type=F32)
    qs = jnp.dot(a, ws_ref[...], preferred_element_type=F32)
    cos = jnp.tile(cos_ref[...], (1, reps))
    sin = jnp.tile(sin_ref[...], (1, reps))
    o_ref[...] = ((q * cos + qs * sin) * scale).astype(o_ref.dtype)


def _rope_q(cq, w_r, w_r_swapped, cos_t, sin_t, *, scale, tm=1024, tn=1024):
    m, kd = cq.shape
    n = w_r.shape[1]
    tm, tn = _pick(m, tm), _pick(n, tn)
    return pl.pallas_call(
        functools.partial(_rope_q_kernel, scale=scale, reps=tn // LANES),
        grid=(m // tm, n // tn),
        in_specs=[
            pl.BlockSpec((tm, kd), lambda i, j: (i, 0)),
            pl.BlockSpec((kd, tn), lambda i, j: (0, j)),
            pl.BlockSpec((kd, tn), lambda i, j: (0, j)),
            pl.BlockSpec((tm, LANES), lambda i, j: (i, 0)),
            pl.BlockSpec((tm, LANES), lambda i, j: (i, 0)),
        ],
        out_specs=pl.BlockSpec((tm, tn), lambda i, j: (i, j)),
        out_shape=jax.ShapeDtypeStruct((m, n), BF16),
        compiler_params=_cparams("parallel", "parallel"),
        name="rope_q",
    )(cq, w_r, w_r_swapped, cos_t, sin_t)


ATTN_BLOCK = 256


def _attn_kernel(qn_ref, qr_ref, kv_ref, kr_ref, o_ref, qc_ref, kc_ref, s_ref, p_ref, *, seq):
    blk = ATTN_BLOCK
    nt = (((1,), (1,)), ((), ()))
    qk = QK_NOPE_DIM + QK_ROPE_DIM
    hw = QK_NOPE_DIM + V_HEAD_DIM
    row = lax.broadcasted_iota(jnp.int32, (blk, blk), 0)
    col = lax.broadcasted_iota(jnp.int32, (blk, blk), 1)
    for h in range(2):
        qc_ref[:, :QK_NOPE_DIM] = qn_ref[:, h * QK_NOPE_DIM:(h + 1) * QK_NOPE_DIM]
        qc_ref[:, QK_NOPE_DIM:qk] = qr_ref[:, h * QK_ROPE_DIM:(h + 1) * QK_ROPE_DIM]
        kc_ref[:, :QK_NOPE_DIM] = kv_ref[:, h * hw:h * hw + QK_NOPE_DIM]
        kc_ref[:, QK_NOPE_DIM:qk] = kr_ref[...]
        for i in range(seq // blk):
            par = i % 2
            q = qc_ref[i * blk:(i + 1) * blk, :]
            m = None
            for j in range(i + 1):
                s = lax.dot_general(q, kc_ref[j * blk:(j + 1) * blk, :], nt, preferred_element_type=F32)
                if j == i:
                    s = jnp.where(col <= row, s, -jnp.inf)
                s_ref[par, :, j * blk:(j + 1) * blk] = s
                mj = jnp.max(s, axis=-1, keepdims=True)
                m = mj if m is None else jnp.maximum(m, mj)
            l = None
            for j in range(i + 1):
                p = jnp.exp(s_ref[par, :, j * blk:(j + 1) * blk] - m)
                lj = jnp.sum(p, axis=-1, keepdims=True)
                l = lj if l is None else l + lj
                p_ref[par, :, j * blk:(j + 1) * blk] = p.astype(BF16)
            w = (i + 1) * blk
            v = kv_ref[0:w, h * hw + QK_NOPE_DIM:(h + 1) * hw]
            acc = jnp.dot(p_ref[par, :, 0:w], v, preferred_element_type=F32)
            o_ref[i * blk:(i + 1) * blk, h * V_HEAD_DIM:(h + 1) * V_HEAD_DIM] = (acc / l).astype(o_ref.dtype)


def _attention(qn, qr, kv, kr, *, batch, seq):
    t = qn.shape[0]
    hp = N_HEADS // 2
    kvw = 2 * (QK_NOPE_DIM + V_HEAD_DIM)
    qk = QK_NOPE_DIM + QK_ROPE_DIM
    return pl.pallas_call(
        functools.partial(_attn_kernel, seq=seq),
        grid=(batch, hp),
        in_specs=[
            pl.BlockSpec((seq, 2 * QK_NOPE_DIM), lambda b, g: (b, g)),
            pl.BlockSpec((seq, 2 * QK_ROPE_DIM), lambda b, g: (b, g)),
            pl.BlockSpec((seq, kvw), lambda b, g: (b, g)),
            pl.BlockSpec((seq, QK_ROPE_DIM), lambda b, g: (b, 0)),
        ],
        out_specs=pl.BlockSpec((seq, 2 * V_HEAD_DIM), lambda b, g: (b, g)),
        out_shape=jax.ShapeDtypeStruct((t, N_HEADS * V_HEAD_DIM), BF16),
        scratch_shapes=[
            pltpu.VMEM((seq, qk), BF16),
            pltpu.VMEM((seq, qk), BF16),
            pltpu.VMEM((2, ATTN_BLOCK, seq), F32),
            pltpu.VMEM((2, ATTN_BLOCK, seq), BF16),
        ],
        compiler_params=_cparams("parallel", "parallel"),
        name="attention",
    )(qn, qr, kv, kr)


def _split_hi_lo(w):
    hi = w.astype(BF16)
    lo = (w - hi.astype(F32)).astype(BF16)
    return hi, lo


def _swap_halves(w, width):
    k, n = w.shape
    return w.reshape(k, n // width, 2, width // 2)[:, :, ::-1, :].reshape(k, n)


def _moe_layer(x, x_slabs, logits_t, router_b, w_in, w_down, layer, g, b):
    t, d = x.shape
    tm = MOE_ROW_TILE
    experts, gate_w, rank, counts = _route(logits_t, router_b)

    cnt = counts[:, 0].astype(jnp.int32)
    tiles_per_expert = (cnt + tm - 1) // tm
    eid = jnp.arange(N_EXPERTS, dtype=jnp.int32)
    tile_end = jnp.sum(jnp.where(eid[None, :] <= eid[:, None], tiles_per_expert[None, :], 0), axis=1)
    row_off = (tile_end - tiles_per_expert) * tm
    n_tiles = (2 * t) // tm + N_EXPERTS
    n_rows = n_tiles * tm
    tile_id = jnp.arange(n_tiles, dtype=jnp.int32)
    n_used = tile_end[-1].astype(jnp.int32)
    tile_valid = jnp.concatenate([(tile_id < n_used).astype(jnp.int32), n_used[None]])
    last_tile = jnp.maximum(n_used - 1, 0)
    before = tile_end[None, :] <= jnp.minimum(tile_id, last_tile)[:, None]
    tile_expert = jnp.minimum(jnp.sum(before.astype(jnp.int32), axis=1), N_EXPERTS - 1)
    onehot = experts[:, :, None] == eid
    pos = jnp.sum(jnp.where(onehot, row_off, 0), axis=-1) + rank
    tok = jnp.broadcast_to(jnp.arange(t, dtype=jnp.int32)[None, :], (2, t))
    src_tok = jnp.zeros((n_rows,), jnp.int32).at[pos.reshape(-1)].set(tok.reshape(-1))

    hmid = _ffn_up(x_slabs, src_tok, w_in, layer, tile_expert, tile_valid, tm=tm)
    ys = _ffn_down(hmid, w_down, layer, tile_expert, tile_valid, tm=tm)
    return _combine_ln(ys, pos, gate_w.T, x, g, b)


def kernel(x, p, positions, conv_w_in, conv_b_in, conv_w_dw, conv_b_dw, conv_ln_g, conv_ln_b, conv_w_out, conv_b_out, kv_w_down, kv_norm_g, kv_w_up, q_w_down, q_norm_g, q_w_up, attn_w_out, router_w, router_b, moe_w_in, moe_w_down, ln1_g, ln1_b, ln2_g, ln2_b, ple_w_gate, ple_b_gate, ple_w_proj):
    batch, seq, d = x.shape
    t = batch * seq
    row = lambda v: v.reshape(1, -1).astype(F32)

    x0 = x.reshape(t, d)
    rw_hi, rw_lo = _split_hi_lo(router_w.T)
    pb = p.reshape(DEPTH, t, -1).astype(BF16)

    glu = _glu_matmul(x0.astype(BF16), conv_w_in[0].astype(BF16), row(conv_b_in[0]))
    hn = _conv_ln_swish(glu, conv_w_dw[0].reshape(CONV_KERNEL, d), row(conv_b_dw[0]),
                        row(conv_ln_g[0]), row(conv_ln_b[0]), batch=batch, seq=seq)
    mix = _matmul(hn, conv_w_out[0].astype(BF16), row(conv_b_out[0]), wide_out=True)
    x1, x1s, lg = _ln_router(mix, x0, row(ln1_g[0]), row(ln1_b[0]), rw_hi, rw_lo)
    x2, x2b = _moe_layer(x1, x1s, lg, router_b, moe_w_in, moe_w_down, 0, row(ln2_g[0]), row(ln2_b[0]))
    x3, x3b = _per_layer_input(x2b, x2, pb[0], ple_w_gate[0].astype(BF16), row(ple_b_gate[0]),
                               ple_w_proj[0].astype(BF16))

    cos_t, sin_t = _rope_tables(positions)
    qr_rank = q_w_down.shape[2]
    w_kpe = kv_w_down[:, KV_LORA_RANK:]
    w_cat = jnp.concatenate([q_w_down[0], kv_w_down[:, :KV_LORA_RANK], w_kpe,
                             _swap_halves(w_kpe, QK_ROPE_DIM)], axis=1).astype(BF16)
    cq, ckv, k_rot = _latents(x3b, w_cat, row(q_norm_g[0]), row(kv_norm_g), cos_t, sin_t, qr=qr_rank)
    kv = _matmul(ckv, kv_w_up.reshape(KV_LORA_RANK, -1).astype(BF16))

    w_qn = q_w_up[0][:, :, :QK_NOPE_DIM].reshape(qr_rank, -1).astype(BF16)
    w_qr = q_w_up[0][:, :, QK_NOPE_DIM:].reshape(qr_rank, -1)
    qn = _matmul(cq, w_qn, scale=SOFTMAX_SCALE)
    q_rot = _rope_q(cq, w_qr.astype(BF16), _swap_halves(w_qr, QK_ROPE_DIM).astype(BF16), cos_t, sin_t,
                    scale=SOFTMAX_SCALE)
    o = _attention(qn, q_rot, kv, k_rot, batch=batch, seq=seq)
    mix = _matmul(o, attn_w_out[0].astype(BF16), wide_out=True)
    x4, x4s, lg = _ln_router(mix, x3, row(ln1_g[1]), row(ln1_b[1]), rw_hi, rw_lo)
    x5, x5b = _moe_layer(x4, x4s, lg, router_b, moe_w_in, moe_w_down, 1, row(ln2_g[1]), row(ln2_b[1]))
    x6, _ = _per_layer_input(x5b, x5, pb[1], ple_w_gate[1].astype(BF16), row(ple_b_gate[1]),
                             ple_w_proj[1].astype(BF16))
    return x6.reshape(batch, seq, d)
```

```python
import functools

import jax
import jax.numpy as jnp
from jax import lax
from jax.experimental import pallas as pl
from jax.experimental.pallas import tpu as pltpu

F32 = jnp.float32
BF16 = jnp.bfloat16

CONV_KERNEL = 31
N_HEADS = 32
QK_NOPE_DIM = 128
QK_ROPE_DIM = 64
V_HEAD_DIM = 128
KV_LORA_RANK = 512
ROPE_THETA = 10000.0
SOFTMAX_SCALE = (QK_NOPE_DIM + QK_ROPE_DIM) ** -0.5
N_EXPERTS = 32
N_GROUPS = 8
EXPERTS_PER_GROUP = N_EXPERTS // N_GROUPS
DEPTH = 2
ALPHA = (2 * DEPTH) ** 0.25
LN_EPS = 1e-5
RMS_EPS = 1e-6

LANES = 128
SUBLANES = 8
MOE_ROW_TILE = 256
WAIT_UNROLL = 8
ISSUE_UNROLL = 4
VMEM_LIMIT = 56 * 1024 * 1024


def _cparams(*sem, vmem=VMEM_LIMIT):
    return pltpu.CompilerParams(dimension_semantics=sem, vmem_limit_bytes=vmem)


def _pick(n, pref):
    t = min(n, pref)
    while n % t:
        t //= 2
    return t


def _sigmoid(x):
    return 1.0 / (1.0 + jnp.exp(-x))


def _layer_norm(r, g, b):
    mu = jnp.mean(r, axis=-1, keepdims=True)
    rc = r - mu
    var = jnp.mean(rc * rc, axis=-1, keepdims=True)
    return rc * lax.rsqrt(var + LN_EPS) * g + b


def _mm_kernel(a_ref, w_ref, b_ref, o_ref, *, scale):
    acc = jnp.dot(a_ref[...], w_ref[...], preferred_element_type=F32)
    o_ref[...] = ((acc + b_ref[...]) * scale).astype(o_ref.dtype)


def _matmul(a, w, bias=None, *, scale=1.0, wide_out=False, tm=1024, tn=1024):
    m, kd = a.shape
    n = w.shape[1]
    tm, tn = _pick(m, tm), _pick(n, tn)
    if bias is None:
        bias = jnp.zeros((1, n), F32)
    return pl.pallas_call(
        functools.partial(_mm_kernel, scale=scale),
        grid=(m // tm, n // tn),
        in_specs=[
            pl.BlockSpec((tm, kd), lambda i, j: (i, 0)),
            pl.BlockSpec((kd, tn), lambda i, j: (0, j)),
            pl.BlockSpec((1, tn), lambda i, j: (0, j)),
        ],
        out_specs=pl.BlockSpec((tm, tn), lambda i, j: (i, j)),
        out_shape=jax.ShapeDtypeStruct((m, n), F32 if wide_out else BF16),
        compiler_params=_cparams("parallel", "parallel"),
        name="matmul",
    )(a, w, bias)


def _glu_kernel(a_ref, wa_ref, wg_ref, ba_ref, bg_ref, o_ref):
    a = a_ref[...]
    val = jnp.dot(a, wa_ref[...], preferred_element_type=F32) + ba_ref[...]
    gate = jnp.dot(a, wg_ref[...], preferred_element_type=F32) + bg_ref[...]
    o_ref[...] = (val * _sigmoid(gate)).astype(o_ref.dtype)


def _glu_matmul(a, w, bias, *, tm=1024, tn=512):
    m, kd = a.shape
    n = w.shape[1] // 2
    tm, tn = _pick(m, tm), _pick(n, tn)
    nj = n // tn
    return pl.pallas_call(
        _glu_kernel,
        grid=(m // tm, nj),
        in_specs=[
            pl.BlockSpec((tm, kd), lambda i, j: (i, 0)),
            pl.BlockSpec((kd, tn), lambda i, j: (0, j)),
            pl.BlockSpec((kd, tn), lambda i, j: (0, j + nj)),
            pl.BlockSpec((1, tn), lambda i, j: (0, j)),
            pl.BlockSpec((1, tn), lambda i, j: (0, j + nj)),
        ],
        out_specs=pl.BlockSpec((tm, tn), lambda i, j: (i, j)),
        out_shape=jax.ShapeDtypeStruct((m, n), F32),
        compiler_params=_cparams("parallel", "parallel"),
        name="glu_matmul",
    )(a, w, w, bias, bias)


CONV_HALO = 32
CONV_ROWS = 64
CONV_LEAD = CONV_HALO - (CONV_KERNEL - 1)


def _conv_ln_kernel(prev_ref, cur_ref, w_ref, bdw_ref, g_ref, b_ref, o_ref, win_ref, conv_ref, *, ts, d):
    i = pl.program_id(1)

    @pl.when(i == 0)
    def _():
        win_ref[0:CONV_HALO, :] = jnp.zeros((CONV_HALO, d), F32)

    @pl.when(i > 0)
    def _():
        win_ref[0:CONV_HALO, :] = prev_ref[...]

    win_ref[CONV_HALO:CONV_HALO + ts, :] = cur_ref[...]
    win_ref[CONV_HALO + ts:, :] = jnp.zeros((SUBLANES, d), F32)

    def cols_body(cb, carry):
        cols = pl.ds(pl.multiple_of(cb * LANES, LANES), LANES)
        for r0 in range(0, ts, CONV_ROWS):
            acc = None
            for c in range(SUBLANES):
                u = None
                for a in range((CONV_LEAD + CONV_KERNEL - 1) // SUBLANES + 1):
                    k = SUBLANES * a + c - CONV_LEAD
                    if 0 <= k < CONV_KERNEL:
                        term = win_ref[pl.ds(r0 + SUBLANES * a, CONV_ROWS + SUBLANES), cols] * w_ref[k:k + 1, cols]
                        u = term if u is None else u + term
                part = u[c:c + CONV_ROWS]
                acc = part if acc is None else acc + part
            conv_ref[pl.ds(r0, CONV_ROWS), cols] = acc
        return carry

    lax.fori_loop(0, d // LANES, cols_body, 0)

    h = _layer_norm(conv_ref[...] + bdw_ref[...], g_ref[...], b_ref[...])
    o_ref[...] = (h * _sigmoid(h)).astype(o_ref.dtype)


def _conv_ln_swish(h, w_dw, b_dw, g, b, *, batch, seq, ts=256):
    t, d = h.shape
    ts = _pick(seq, ts)
    nblk = seq // ts
    hb = ts // CONV_HALO
    return pl.pallas_call(
        functools.partial(_conv_ln_kernel, ts=ts, d=d),
        grid=(batch, nblk),
        in_specs=[
            pl.BlockSpec((CONV_HALO, d), lambda bi, i: (jnp.maximum((bi * nblk + i) * hb - 1, 0), 0)),
            pl.BlockSpec((ts, d), lambda bi, i: (bi * nblk + i, 0)),
            pl.BlockSpec((CONV_KERNEL, d), lambda bi, i: (0, 0)),
            pl.BlockSpec((1, d), lambda bi, i: (0, 0)),
            pl.BlockSpec((1, d), lambda bi, i: (0, 0)),
            pl.BlockSpec((1, d), lambda bi, i: (0, 0)),
        ],
        out_specs=pl.BlockSpec((ts, d), lambda bi, i: (bi * nblk + i, 0)),
        out_shape=jax.ShapeDtypeStruct((t, d), BF16),
        scratch_shapes=[pltpu.VMEM((CONV_HALO + ts + SUBLANES, d), F32), pltpu.VMEM((ts, d), F32)],
        compiler_params=_cparams("parallel", "arbitrary"),
        name="conv_ln_swish",
    )(h, h, w_dw, b_dw, g, b)


def _slab_pitch(nslab):
    pitch = -(-nslab // SUBLANES) * SUBLANES
    return pitch if (pitch // SUBLANES) % 2 else pitch + SUBLANES


def _store_slabs(slab_ref, y, first_slab, tm, pitch):
    for j in range(y.shape[1] // LANES):
        slab_ref[pl.ds(first_slab + j, tm, stride=pitch), :] = y[:, j * LANES:(j + 1) * LANES]


def _zero_pad_slabs(slab_ref, nslab, tm, pitch):
    for j in range(nslab, pitch):
        slab_ref[pl.ds(j, tm, stride=pitch), :] = jnp.zeros((tm, LANES), F32)


def _load_slabs(slab_ref, base, first_slab, count, tm, pitch):
    parts = [slab_ref[pl.ds(base + first_slab + j, tm, stride=pitch), :] for j in range(count)]
    return parts[0] if count == 1 else jnp.concatenate(parts, axis=1)


def _ln_router_kernel(mix_ref, res_ref, g_ref, b_ref, rwh_ref, rwl_ref, x_ref, slab_ref, lg_ref, *, tm, pitch):
    y = _layer_norm(ALPHA * res_ref[...] + mix_ref[...], g_ref[...], b_ref[...])
    x_ref[...] = y
    _store_slabs(slab_ref, y, 0, tm, pitch)
    _zero_pad_slabs(slab_ref, y.shape[1] // LANES, tm, pitch)
    yh = y.astype(BF16)
    yl = (y - yh.astype(F32)).astype(BF16)
    nt = (((1,), (1,)), ((), ()))
    lg = lax.dot_general(rwh_ref[...], yh, nt, preferred_element_type=F32)
    lg = lg + lax.dot_general(rwh_ref[...], yl, nt, preferred_element_type=F32)
    lg = lg + lax.dot_general(rwl_ref[...], yh, nt, preferred_element_type=F32)
    lg_ref[...] = lg


def _ln_router(mix, res, g, b, rw_hi, rw_lo, *, tm=256):
    m, d = mix.shape
    tm = _pick(m, tm)
    ne = rw_hi.shape[0]
    pitch = _slab_pitch(d // LANES)
    row = lambda i: (i, 0)
    const = lambda i: (0, 0)
    return pl.pallas_call(
        functools.partial(_ln_router_kernel, tm=tm, pitch=pitch),
        grid=(m // tm,),
        in_specs=[
            pl.BlockSpec((tm, d), row),
            pl.BlockSpec((tm, d), row),
            pl.BlockSpec((1, d), const),
            pl.BlockSpec((1, d), const),
            pl.BlockSpec((ne, d), const),
            pl.BlockSpec((ne, d), const),
        ],
        out_specs=[
            pl.BlockSpec((tm, d), row),
            pl.BlockSpec((tm * pitch, LANES), row),
            pl.BlockSpec((ne, tm), lambda i: (0, i)),
        ],
        out_shape=[
            jax.ShapeDtypeStruct((m, d), F32),
            jax.ShapeDtypeStruct((m * pitch, LANES), F32),
            jax.ShapeDtypeStruct((ne, m), F32),
        ],
        compiler_params=_cparams("parallel"),
        name="ln_router",
    )(mix, res, g, b, rw_hi, rw_lo)


def _route_kernel(lg_ref, rb_ref, e_ref, w_ref, rank_ref, cnt_ref, carry_ref, *, tt, nsteps):
    step = pl.program_id(0)

    @pl.when(step == 0)
    def _():
        carry_ref[...] = jnp.zeros_like(carry_ref)

    aff = _sigmoid(lg_ref[...])
    biased = aff + rb_ref[...]
    rows_b = [biased[e:e + 1, :] for e in range(N_EXPERTS)]
    rows_a = [aff[e:e + 1, :] for e in range(N_EXPERTS)]

    def top2_sum(v):
        m01, n01 = jnp.maximum(v[0], v[1]), jnp.minimum(v[0], v[1])
        m23, n23 = jnp.maximum(v[2], v[3]), jnp.minimum(v[2], v[3])
        return jnp.maximum(m01, m23) + jnp.maximum(jnp.minimum(m01, m23), jnp.maximum(n01, n23))

    best = top2_sum(rows_b[0:EXPERTS_PER_GROUP])
    gidx = jnp.zeros((1, tt), jnp.int32)
    for g in range(1, N_GROUPS):
        s = top2_sum(rows_b[g * EXPERTS_PER_GROUP:(g + 1) * EXPERTS_PER_GROUP])
        better = s > best
        gidx = jnp.where(better, g, gidx)
        best = jnp.where(better, s, best)

    vb = [rows_b[i] for i in range(EXPERTS_PER_GROUP)]
    va = [rows_a[i] for i in range(EXPERTS_PER_GROUP)]
    for g in range(1, N_GROUPS):
        sel = gidx == g
        for i in range(EXPERTS_PER_GROUP):
            vb[i] = jnp.where(sel, rows_b[g * EXPERTS_PER_GROUP + i], vb[i])
            va[i] = jnp.where(sel, rows_a[g * EXPERTS_PER_GROUP + i], va[i])

    i1 = jnp.zeros((1, tt), jnp.int32)
    b1, a1 = vb[0], va[0]
    for i in range(1, EXPERTS_PER_GROUP):
        better = vb[i] > b1
        i1 = jnp.where(better, i, i1)
        b1 = jnp.where(better, vb[i], b1)
        a1 = jnp.where(better, va[i], a1)
    i2 = jnp.where(i1 == 0, 1, 0).astype(jnp.int32)
    b2 = jnp.where(i1 == 0, vb[1], vb[0])
    a2 = jnp.where(i1 == 0, va[1], va[0])
    for i in range(1, EXPERTS_PER_GROUP):
        better = jnp.logical_and(vb[i] > b2, i1 != i)
        i2 = jnp.where(better, i, i2)
        b2 = jnp.where(better, vb[i], b2)
        a2 = jnp.where(better, va[i], a2)

    e1 = gidx * EXPERTS_PER_GROUP + i1
    e2 = gidx * EXPERTS_PER_GROUP + i2
    den = a1 + a2
    e_ref[0:1, :] = e1
    e_ref[1:2, :] = e2
    w_ref[0:1, :] = a1 / den
    w_ref[1:2, :] = a2 / den

    eid = lax.broadcasted_iota(jnp.int32, (N_EXPERTS, tt), 0)
    hit1 = eid == e1
    hit2 = eid == e2
    onehot = jnp.where(jnp.logical_or(hit1, hit2), 1.0, 0.0)
    src = lax.broadcasted_iota(jnp.int32, (tt, tt), 0)
    dst = lax.broadcasted_iota(jnp.int32, (tt, tt), 1)
    upper = jnp.where(src < dst, 1.0, 0.0).astype(BF16)
    before = jnp.dot(onehot.astype(BF16), upper, preferred_element_type=F32) + carry_ref[:, 0:1]
    rank_ref[0:1, :] = jnp.sum(jnp.where(hit1, before, 0.0), axis=0, keepdims=True).astype(jnp.int32)
    rank_ref[1:2, :] = jnp.sum(jnp.where(hit2, before, 0.0), axis=0, keepdims=True).astype(jnp.int32)
    carry_ref[...] += jnp.sum(onehot, axis=1, keepdims=True)

    @pl.when(step == nsteps - 1)
    def _():
        cnt_ref[...] = carry_ref[...]


def _route(logits_t, router_b, *, tt=512):
    ne, t = logits_t.shape
    tt = _pick(t, tt)
    nsteps = t // tt
    return pl.pallas_call(
        functools.partial(_route_kernel, tt=tt, nsteps=nsteps),
        grid=(nsteps,),
        in_specs=[
            pl.BlockSpec((ne, tt), lambda s: (0, s)),
            pl.BlockSpec((ne, 1), lambda s: (0, 0)),
        ],
        out_specs=[
            pl.BlockSpec((2, tt), lambda s: (0, s)),
            pl.BlockSpec((2, tt), lambda s: (0, s)),
            pl.BlockSpec((2, tt), lambda s: (0, s)),
            pl.BlockSpec((ne, LANES), lambda s: (0, 0)),
        ],
        out_shape=[
            jax.ShapeDtypeStruct((2, t), jnp.int32),
            jax.ShapeDtypeStruct((2, t), F32),
            jax.ShapeDtypeStruct((2, t), jnp.int32),
            jax.ShapeDtypeStruct((ne, LANES), F32),
        ],
        scratch_shapes=[pltpu.VMEM((ne, LANES), F32)],
        compiler_params=_cparams("arbitrary"),
        name="route",
    )(logits_t, router_b.reshape(ne, 1).astype(F32))


def _token_copy(src_ref, dst_ref, sem, src_tok, dst_tok, nslab, pitch):
    src = src_ref.at[pl.ds(pl.multiple_of(src_tok * pitch, SUBLANES), nslab)]
    dst = dst_ref.at[pl.ds(pl.multiple_of(dst_tok * pitch, SUBLANES), nslab)]
    return pltpu.make_async_copy(src, dst, sem)


WEIGHT_CAST_ROWS = 256


def _refresh_expert_weights(w_hbm, stage_ref, wres_ref, wsem, layer, expert, next_expert):
    pltpu.make_async_copy(w_hbm.at[layer, expert], stage_ref, wsem).wait()
    rows = stage_ref.shape[0]
    step = min(WEIGHT_CAST_ROWS, rows)
    for r0 in range(0, rows, step):
        wres_ref[r0:r0 + step, :] = stage_ref[r0:r0 + step, :].astype(BF16)

    @pl.when(next_expert >= 0)
    def _():
        pltpu.make_async_copy(w_hbm.at[layer, next_expert], stage_ref, wsem).start()


def _ffn_up_kernel(te_ref, tv_ref, first_ref, ne_ref, cur_ref, nxt_ref, slab_ref, w_hbm, o_ref,
                   buf_ref, sem, stage_ref, wres_ref, wsem, *, layer, nk, tm, n_tiles, nslab, pitch):
    i = pl.program_id(0)
    n_used = tv_ref[n_tiles]
    used = i < n_used
    slot = lax.rem(i, 2)
    spk = nslab // nk
    de = wres_ref.shape[1] // 2

    @pl.when(jnp.logical_and(used, i == 0))
    def _():
        pltpu.make_async_copy(w_hbm.at[layer, te_ref[0]], stage_ref, wsem).start()

    def issue(idx_ref, dst_slot):
        def body(r, carry):
            _token_copy(slab_ref, buf_ref, sem.at[dst_slot], idx_ref[0, 0, r], dst_slot * tm + r,
                        nslab, pitch).start()
            return carry

        lax.fori_loop(0, tm, body, 0, unroll=ISSUE_UNROLL)

    @pl.when(jnp.logical_and(used, i == 0))
    def _():
        issue(cur_ref, 0)

    @pl.when(i + 1 < n_used)
    def _():
        issue(nxt_ref, 1 - slot)

    @pl.when(jnp.logical_and(used, first_ref[i] > 0))
    def _():
        _refresh_expert_weights(w_hbm, stage_ref, wres_ref, wsem, layer, te_ref[i], ne_ref[i])

    @pl.when(used)
    def _():
        def drain(r, carry):
            _token_copy(slab_ref, buf_ref, sem.at[slot], 0, slot * tm + r, nslab, pitch).wait()
            return carry

        lax.fori_loop(0, tm, drain, 0, unroll=WAIT_UNROLL)
        h = None
        for k in range(nk):
            x = _load_slabs(buf_ref, slot * (tm * pitch), k * spk, spk, tm, pitch).astype(BF16)
            part = jnp.dot(x, wres_ref[k * spk * LANES:(k + 1) * spk * LANES, :], preferred_element_type=F32)
            h = part if h is None else h + part
        hg = h[:, :de]
        o_ref[...] = (hg * _sigmoid(hg) * h[:, de:]).astype(o_ref.dtype)

    @pl.when(jnp.logical_not(used))
    def _():
        o_ref[...] = jnp.zeros_like(o_ref)


def _ffn_up(slabs, src_tok, w_in, layer, plan, *, tm, tk=1024):
    d = w_in.shape[2]
    de2 = w_in.shape[3]
    nslab = d // LANES
    pitch = _slab_pitch(nslab)
    tk = _pick(d, tk)
    nk = d // tk
    n_tiles = src_tok.shape[0] // tm
    idx = src_tok.reshape(n_tiles, 1, tm)

    grid_spec = pltpu.PrefetchScalarGridSpec(
        num_scalar_prefetch=4,
        grid=(n_tiles,),
        in_specs=[
            pl.BlockSpec((1, 1, tm), lambda i, *_: (i, 0, 0), memory_space=pltpu.SMEM),
            pl.BlockSpec((1, 1, tm), lambda i, *_: (jnp.minimum(i + 1, n_tiles - 1), 0, 0),
                         memory_space=pltpu.SMEM),
            pl.BlockSpec(memory_space=pl.ANY),
            pl.BlockSpec(memory_space=pl.ANY),
        ],
        out_specs=pl.BlockSpec((tm, de2 // 2), lambda i, *_: (i, 0)),
        scratch_shapes=[
            pltpu.VMEM((2 * tm * pitch, LANES), F32),
            pltpu.SemaphoreType.DMA((2,)),
            pltpu.VMEM((d, de2), F32),
            pltpu.VMEM((d, de2), BF16),
            pltpu.SemaphoreType.DMA,
        ],
    )
    return pl.pallas_call(
        functools.partial(_ffn_up_kernel, layer=layer, nk=nk, tm=tm, n_tiles=n_tiles, nslab=nslab, pitch=pitch),
        grid_spec=grid_spec,
        out_shape=jax.ShapeDtypeStruct((n_tiles * tm, de2 // 2), BF16),
        compiler_params=_cparams("arbitrary"),
        name="ffn_up",
    )(*plan, idx, idx, slabs, w_in)


def _ffn_down_kernel(te_ref, tv_ref, first_ref, ne_ref, h_ref, w_hbm, o_ref, stage_ref, wres_ref, wsem,
                     *, layer, tm, nslab, pitch, spj):
    i = pl.program_id(0)
    valid = tv_ref[i] > 0

    @pl.when(jnp.logical_and(valid, i == 0))
    def _():
        pltpu.make_async_copy(w_hbm.at[layer, te_ref[0]], stage_ref, wsem).start()

    @pl.when(jnp.logical_and(valid, first_ref[i] > 0))
    def _():
        _refresh_expert_weights(w_hbm, stage_ref, wres_ref, wsem, layer, te_ref[i], ne_ref[i])

    @pl.when(valid)
    def _():
        h = h_ref[...]
        for j in range(nslab // spj):
            y = jnp.dot(h, wres_ref[:, j * spj * LANES:(j + 1) * spj * LANES], preferred_element_type=F32)
            _store_slabs(o_ref, y, j * spj, tm, pitch)
        _zero_pad_slabs(o_ref, nslab, tm, pitch)

    @pl.when(jnp.logical_not(valid))
    def _():
        o_ref[...] = jnp.zeros_like(o_ref)


def _ffn_down(h, w_down, layer, plan, *, tm, tn=1024):
    r, de = h.shape
    d = w_down.shape[3]
    nslab = d // LANES
    pitch = _slab_pitch(nslab)
    tn = _pick(d, tn)
    n_tiles = r // tm

    def ii(i, tv):
        return jnp.maximum(jnp.minimum(i, tv[n_tiles] - 1), 0)

    grid_spec = pltpu.PrefetchScalarGridSpec(
        num_scalar_prefetch=4,
        grid=(n_tiles,),
        in_specs=[
            pl.BlockSpec((tm, de), lambda i, te, tv, *_: (ii(i, tv), 0)),
            pl.BlockSpec(memory_space=pl.ANY),
        ],
        out_specs=pl.BlockSpec((tm * pitch, LANES), lambda i, *_: (i, 0)),
        scratch_shapes=[
            pltpu.VMEM((de, d), F32),
            pltpu.VMEM((de, d), BF16),
            pltpu.SemaphoreType.DMA,
        ],
    )
    return pl.pallas_call(
        functools.partial(_ffn_down_kernel, layer=layer, tm=tm, nslab=nslab, pitch=pitch, spj=tn // LANES),
        grid_spec=grid_spec,
        out_shape=jax.ShapeDtypeStruct((r * pitch, LANES), F32),
        compiler_params=_cparams("arbitrary"),
        name="ffn_down",
    )(*plan, h, w_down)


def _combine_ln_kernel(cur_ref, nxt_ref, ys_ref, gw_ref, x_ref, g_ref, b_ref, xo_ref, xb_ref, buf_ref, sem,
                       *, tm, nb, nslab, pitch):
    i = pl.program_id(0)
    slot = lax.rem(i, 2)

    def copies(idx_ref, dst_slot, r):
        first = _token_copy(ys_ref, buf_ref, sem.at[dst_slot], idx_ref[0, 0, r],
                            (2 * dst_slot) * tm + r, nslab, pitch)
        second = _token_copy(ys_ref, buf_ref, sem.at[dst_slot], idx_ref[0, 0, tm + r],
                             (2 * dst_slot + 1) * tm + r, nslab, pitch)
        return first, second

    def issue(idx_ref, dst_slot):
        def body(r, carry):
            first, second = copies(idx_ref, dst_slot, r)
            first.start()
            second.start()
            return carry

        lax.fori_loop(0, tm, body, 0, unroll=ISSUE_UNROLL)

    @pl.when(i == 0)
    def _():
        issue(cur_ref, 0)

    @pl.when(i + 1 < nb)
    def _():
        issue(nxt_ref, 1 - slot)

    def drain(r, carry):
        first, second = copies(cur_ref, slot, r)
        first.wait()
        second.wait()
        return carry

    lax.fori_loop(0, tm, drain, 0, unroll=WAIT_UNROLL)
    gw = gw_ref[...]
    y0 = _load_slabs(buf_ref, (2 * slot) * (tm * pitch), 0, nslab, tm, pitch)
    y1 = _load_slabs(buf_ref, (2 * slot + 1) * (tm * pitch), 0, nslab, tm, pitch)
    y = gw[:, 0:1] * y0 + gw[:, 1:2] * y1
    out = _layer_norm(ALPHA * x_ref[...] + y, g_ref[...], b_ref[...])
    xo_ref[...] = out
    xb_ref[...] = out.astype(BF16)


def _combine_ln(ys, pos, gate_w, x, g, b, *, tm=256):
    t, d = x.shape
    tm = _pick(t, tm)
    nb = t // tm
    nslab = d // LANES
    pitch = _slab_pitch(nslab)
    pos_tiles = pos.reshape(2, nb, tm).transpose(1, 0, 2).reshape(nb, 1, 2 * tm)
    row = lambda i: (i, 0)
    const = lambda i: (0, 0)
    return pl.pallas_call(
        functools.partial(_combine_ln_kernel, tm=tm, nb=nb, nslab=nslab, pitch=pitch),
        grid=(nb,),
        in_specs=[
            pl.BlockSpec((1, 1, 2 * tm), lambda i: (i, 0, 0), memory_space=pltpu.SMEM),
            pl.BlockSpec((1, 1, 2 * tm), lambda i: (jnp.minimum(i + 1, nb - 1), 0, 0), memory_space=pltpu.SMEM),
            pl.BlockSpec(memory_space=pl.ANY),
            pl.BlockSpec((tm, 2), row),
            pl.BlockSpec((tm, d), row),
            pl.BlockSpec((1, d), const),
            pl.BlockSpec((1, d), const),
        ],
        out_specs=[pl.BlockSpec((tm, d), row), pl.BlockSpec((tm, d), row)],
        out_shape=[jax.ShapeDtypeStruct((t, d), F32), jax.ShapeDtypeStruct((t, d), BF16)],
        scratch_shapes=[pltpu.VMEM((4 * tm * pitch, LANES), F32), pltpu.SemaphoreType.DMA((2,))],
        compiler_params=_cparams("arbitrary"),
        name="combine_ln",
    )(pos_tiles, pos_tiles, ys, gate_w, x, g, b)


def _ple_kernel(a_ref, wg_ref, bg_ref, p_ref, wp_ref, x_ref, xo_ref, xb_ref):
    gate = _sigmoid(jnp.dot(a_ref[...], wg_ref[...], preferred_element_type=F32) + bg_ref[...])
    proj = jnp.dot(p_ref[...], wp_ref[...], preferred_element_type=F32)
    out = x_ref[...] + gate * proj
    xo_ref[...] = out
    xb_ref[...] = out.astype(BF16)


def _per_layer_input(xb, x, p, w_gate, b_gate, w_proj, *, tm=1024, tn=512):
    m, d = x.shape
    pd = p.shape[1]
    tm, tn = _pick(m, tm), _pick(d, tn)
    tile = lambda i, j: (i, j)
    return pl.pallas_call(
        _ple_kernel,
        grid=(m // tm, d // tn),
        in_specs=[
            pl.BlockSpec((tm, d), lambda i, j: (i, 0)),
            pl.BlockSpec((d, tn), lambda i, j: (0, j)),
            pl.BlockSpec((1, tn), lambda i, j: (0, j)),
            pl.BlockSpec((tm, pd), lambda i, j: (i, 0)),
            pl.BlockSpec((pd, tn), lambda i, j: (0, j)),
            pl.BlockSpec((tm, tn), tile),
        ],
        out_specs=[pl.BlockSpec((tm, tn), tile), pl.BlockSpec((tm, tn), tile)],
        out_shape=[jax.ShapeDtypeStruct((m, d), F32), jax.ShapeDtypeStruct((m, d), BF16)],
        compiler_params=_cparams("parallel", "parallel"),
        name="per_layer_input",
    )(xb, w_gate, b_gate, p, w_proj, x)


def _rope_table_kernel(pos_ref, cos_ref, sin_ref):
    half = QK_ROPE_DIM // 2
    lane = lax.broadcasted_iota(jnp.int32, (1, LANES), 1)
    freq = (lane % half).astype(F32)
    inv_freq = 1.0 / (ROPE_THETA ** (freq * (2.0 / QK_ROPE_DIM)))
    ang = pos_ref[...].astype(F32) * inv_freq
    sign = jnp.where((lane // half) % 2 == 0, -1.0, 1.0)
    cos_ref[...] = jnp.cos(ang)
    sin_ref[...] = jnp.sin(ang) * sign


def _rope_tables(positions, *, tm=2048):
    t = positions.size
    tm = _pick(t, tm)
    return pl.pallas_call(
        _rope_table_kernel,
        grid=(t // tm,),
        in_specs=[pl.BlockSpec((tm, 1), lambda i: (i, 0))],
        out_specs=[pl.BlockSpec((tm, LANES), lambda i: (i, 0))] * 2,
        out_shape=[jax.ShapeDtypeStruct((t, LANES), F32)] * 2,
        compiler_params=_cparams("parallel"),
        name="rope_tables",
    )(positions.reshape(t, 1))


def _latent_kernel(a_ref, w_ref, gq_ref, gkv_ref, cos_ref, sin_ref, cq_ref, ckv_ref, kr_ref, *, qr):
    c = jnp.dot(a_ref[...], w_ref[...], preferred_element_type=F32)
    cq = c[:, :qr]
    cq_ref[...] = (cq * lax.rsqrt(jnp.mean(cq * cq, axis=-1, keepdims=True) + RMS_EPS)
                   * gq_ref[...]).astype(BF16)
    ckv = c[:, qr:qr + KV_LORA_RANK]
    ckv_ref[...] = (ckv * lax.rsqrt(jnp.mean(ckv * ckv, axis=-1, keepdims=True) + RMS_EPS)
                    * gkv_ref[...]).astype(BF16)
    kpe = c[:, qr + KV_LORA_RANK:]
    cos = cos_ref[...]
    sin = sin_ref[...]
    rot = kpe[:, :QK_ROPE_DIM] * cos[:, :QK_ROPE_DIM] + kpe[:, QK_ROPE_DIM:] * sin[:, :QK_ROPE_DIM]
    kr_ref[...] = rot.astype(BF16)


def _latents(xb, w_cat, gq, gkv, cos_t, sin_t, *, qr, tm=512):
    m, kd = xb.shape
    n = w_cat.shape[1]
    tm = _pick(m, tm)
    row = lambda i: (i, 0)
    const = lambda i: (0, 0)
    return pl.pallas_call(
        functools.partial(_latent_kernel, qr=qr),
        grid=(m // tm,),
        in_specs=[
            pl.BlockSpec((tm, kd), row),
            pl.BlockSpec((kd, n), const),
            pl.BlockSpec((1, qr), const),
            pl.BlockSpec((1, KV_LORA_RANK), const),
            pl.BlockSpec((tm, LANES), row),
            pl.BlockSpec((tm, LANES), row),
        ],
        out_specs=[
            pl.BlockSpec((tm, qr), row),
            pl.BlockSpec((tm, KV_LORA_RANK), row),
            pl.BlockSpec((tm, QK_ROPE_DIM), row),
        ],
        out_shape=[
            jax.ShapeDtypeStruct((m, qr), BF16),
            jax.ShapeDtypeStruct((m, KV_LORA_RANK), BF16),
            jax.ShapeDtypeStruct((m, QK_ROPE_DIM), BF16),
        ],
        compiler_params=_cparams("parallel"),
        name="latents",
    )(xb, w_cat, gq, gkv, cos_t, sin_t)


def _rope_q_kernel(a_ref, w_ref, ws_ref, cos_ref, sin_ref, o_ref, *, scale, reps):
    a = a_ref[...]
    q = jnp.dot(a, w_ref[...], preferred_element_type=F32)
    qs = jnp.dot(a, ws_ref[...], preferred_element_type=F32)
    cos = jnp.tile(cos_ref[...], (1, reps))
    sin = jnp.tile(sin_ref[...], (1, reps))
    o_ref[...] = ((q * cos + qs * sin) * scale).astype(o_ref.dtype)


def _rope_q(cq, w_r, w_r_swapped, cos_t, sin_t, *, scale, tm=1024, tn=1024):
    m, kd = cq.shape
    n = w_r.shape[1]
    tm, tn = _pick(m, tm), _pick(n, tn)
    return pl.pallas_call(
        functools.partial(_rope_q_kernel, scale=scale, reps=tn // LANES),
        grid=(m // tm, n // tn),
        in_specs=[
            pl.BlockSpec((tm, kd), lambda i, j: (i, 0)),
            pl.BlockSpec((kd, tn), lambda i, j: (0, j)),
            pl.BlockSpec((kd, tn), lambda i, j: (0, j)),
            pl.BlockSpec((tm, LANES), lambda i, j: (i, 0)),
            pl.BlockSpec((tm, LANES), lambda i, j: (i, 0)),
        ],
        out_specs=pl.BlockSpec((tm, tn), lambda i, j: (i, j)),
        out_shape=jax.ShapeDtypeStruct((m, n), BF16),
        compiler_params=_cparams("parallel", "parallel"),
        name="rope_q",
    )(cq, w_r, w_r_swapped, cos_t, sin_t)


ATTN_BLOCK = 256


def _attn_kernel(qn_ref, qr_ref, kv_ref, kr_ref, o_ref, qc_ref, kc_ref, s_ref, p_ref, *, seq):
    blk = ATTN_BLOCK
    nt = (((1,), (1,)), ((), ()))
    qk = QK_NOPE_DIM + QK_ROPE_DIM
    hw = QK_NOPE_DIM + V_HEAD_DIM
    row = lax.broadcasted_iota(jnp.int32, (blk, blk), 0)
    col = lax.broadcasted_iota(jnp.int32, (blk, blk), 1)
    for h in range(2):
        qc_ref[:, :QK_NOPE_DIM] = qn_ref[:, h * QK_NOPE_DIM:(h + 1) * QK_NOPE_DIM]
        qc_ref[:, QK_NOPE_DIM:qk] = qr_ref[:, h * QK_ROPE_DIM:(h + 1) * QK_ROPE_DIM]
        kc_ref[:, :QK_NOPE_DIM] = kv_ref[:, h * hw:h * hw + QK_NOPE_DIM]
        kc_ref[:, QK_NOPE_DIM:qk] = kr_ref[...]
        for i in range(seq // blk):
            par = i % 2
            q = qc_ref[i * blk:(i + 1) * blk, :]
            m = None
            for j in range(i + 1):
                s = lax.dot_general(q, kc_ref[j * blk:(j + 1) * blk, :], nt, preferred_element_type=F32)
                if j == i:
                    s = jnp.where(col <= row, s, -jnp.inf)
                s_ref[par, :, j * blk:(j + 1) * blk] = s
                mj = jnp.max(s, axis=-1, keepdims=True)
                m = mj if m is None else jnp.maximum(m, mj)
            l = None
            for j in range(i + 1):
                p = jnp.exp(s_ref[par, :, j * blk:(j + 1) * blk] - m)
                lj = jnp.sum(p, axis=-1, keepdims=True)
                l = lj if l is None else l + lj
                p_ref[par, :, j * blk:(j + 1) * blk] = p.astype(BF16)
            w = (i + 1) * blk
            v = kv_ref[0:w, h * hw + QK_NOPE_DIM:(h + 1) * hw]
            acc = jnp.dot(p_ref[par, :, 0:w], v, preferred_element_type=F32)
            o_ref[i * blk:(i + 1) * blk, h * V_HEAD_DIM:(h + 1) * V_HEAD_DIM] = (acc / l).astype(o_ref.dtype)


def _attention(qn, qr, kv, kr, *, batch, seq):
    t = qn.shape[0]
    hp = N_HEADS // 2
    kvw = 2 * (QK_NOPE_DIM + V_HEAD_DIM)
    qk = QK_NOPE_DIM + QK_ROPE_DIM
    return pl.pallas_call(
        functools.partial(_attn_kernel, seq=seq),
        grid=(batch, hp),
        in_specs=[
            pl.BlockSpec((seq, 2 * QK_NOPE_DIM), lambda b, g: (b, g)),
            pl.BlockSpec((seq, 2 * QK_ROPE_DIM), lambda b, g: (b, g)),
            pl.BlockSpec((seq, kvw), lambda b, g: (b, g)),
            pl.BlockSpec((seq, QK_ROPE_DIM), lambda b, g: (b, 0)),
        ],
        out_specs=pl.BlockSpec((seq, 2 * V_HEAD_DIM), lambda b, g: (b, g)),
        out_shape=jax.ShapeDtypeStruct((t, N_HEADS * V_HEAD_DIM), BF16),
        scratch_shapes=[
            pltpu.VMEM((seq, qk), BF16),
            pltpu.VMEM((seq, qk), BF16),
            pltpu.VMEM((2, ATTN_BLOCK, seq), F32),
            pltpu.VMEM((2, ATTN_BLOCK, seq), BF16),
        ],
        compiler_params=_cparams("parallel", "parallel"),
        name="attention",
    )(qn, qr, kv, kr)


def _split_hi_lo(w):
    hi = w.astype(BF16)
    lo = (w - hi.astype(F32)).astype(BF16)
    return hi, lo


def _swap_halves(w, width):
    k, n = w.shape
    return w.reshape(k, n // width, 2, width // 2)[:, :, ::-1, :].reshape(k, n)


def _moe_layer(x, x_slabs, logits_t, router_b, w_in, w_down, layer, g, b):
    t, d = x.shape
    tm = MOE_ROW_TILE
    experts, gate_w, rank, counts = _route(logits_t, router_b)

    cnt = counts[:, 0].astype(jnp.int32)
    tiles_per_expert = (cnt + tm - 1) // tm
    eid = jnp.arange(N_EXPERTS, dtype=jnp.int32)
    tile_end = jnp.sum(jnp.where(eid[None, :] <= eid[:, None], tiles_per_expert[None, :], 0), axis=1)
    row_off = (tile_end - tiles_per_expert) * tm
    n_tiles = (2 * t) // tm + N_EXPERTS
    n_rows = n_tiles * tm
    tile_id = jnp.arange(n_tiles, dtype=jnp.int32)
    n_used = tile_end[-1].astype(jnp.int32)
    tile_valid = jnp.concatenate([(tile_id < n_used).astype(jnp.int32), n_used[None]])
    last_tile = jnp.maximum(n_used - 1, 0)
    before = tile_end[None, :] <= jnp.minimum(tile_id, last_tile)[:, None]
    tile_expert = jnp.minimum(jnp.sum(before.astype(jnp.int32), axis=1), N_EXPERTS - 1)
    prev_expert = jnp.concatenate([jnp.full((1,), -1, jnp.int32), tile_expert[:-1]])
    run_first = jnp.logical_and(tile_id < n_used, tile_expert != prev_expert).astype(jnp.int32)
    later = jnp.logical_and(eid[None, :] > eid[:, None], tiles_per_expert[None, :] > 0)
    next_nonempty = jnp.min(jnp.where(later, eid[None, :], N_EXPERTS), axis=1)
    next_nonempty = jnp.where(next_nonempty < N_EXPERTS, next_nonempty, -1)
    run_next = jnp.sum(jnp.where(tile_expert[:, None] == eid, next_nonempty, 0), axis=1).astype(jnp.int32)
    plan = (tile_expert, tile_valid, run_first, run_next)
    onehot = experts[:, :, None] == eid
    pos = jnp.sum(jnp.where(onehot, row_off, 0), axis=-1) + rank
    tok = jnp.broadcast_to(jnp.arange(t, dtype=jnp.int32)[None, :], (2, t))
    src_tok = jnp.zeros((n_rows,), jnp.int32).at[pos.reshape(-1)].set(tok.reshape(-1))

    hmid = _ffn_up(x_slabs, src_tok, w_in, layer, plan, tm=tm)
    ys = _ffn_down(hmid, w_down, layer, plan, tm=tm)
    return _combine_ln(ys, pos, gate_w.T, x, g, b)


def kernel(x, p, positions, conv_w_in, conv_b_in, conv_w_dw, conv_b_dw, conv_ln_g, conv_ln_b, conv_w_out, conv_b_out, kv_w_down, kv_norm_g, kv_w_up, q_w_down, q_norm_g, q_w_up, attn_w_out, router_w, router_b, moe_w_in, moe_w_down, ln1_g, ln1_b, ln2_g, ln2_b, ple_w_gate, ple_b_gate, ple_w_proj):
    batch, seq, d = x.shape
    t = batch * seq
    row = lambda v: v.reshape(1, -1).astype(F32)

    x0 = x.reshape(t, d)
    rw_hi, rw_lo = _split_hi_lo(router_w.T)
    pb = p.reshape(DEPTH, t, -1).astype(BF16)

    glu = _glu_matmul(x0.astype(BF16), conv_w_in[0].astype(BF16), row(conv_b_in[0]))
    hn = _conv_ln_swish(glu, conv_w_dw[0].reshape(CONV_KERNEL, d), row(conv_b_dw[0]),
                        row(conv_ln_g[0]), row(conv_ln_b[0]), batch=batch, seq=seq)
    mix = _matmul(hn, conv_w_out[0].astype(BF16), row(conv_b_out[0]), wide_out=True)
    x1, x1s, lg = _ln_router(mix, x0, row(ln1_g[0]), row(ln1_b[0]), rw_hi, rw_lo)
    x2, x2b = _moe_layer(x1, x1s, lg, router_b, moe_w_in, moe_w_down, 0, row(ln2_g[0]), row(ln2_b[0]))
    x3, x3b = _per_layer_input(x2b, x2, pb[0], ple_w_gate[0].astype(BF16), row(ple_b_gate[0]),
                               ple_w_proj[0].astype(BF16))

    cos_t, sin_t = _rope_tables(positions)
    qr_rank = q_w_down.shape[2]
    w_kpe = kv_w_down[:, KV_LORA_RANK:]
    w_cat = jnp.concatenate([q_w_down[0], kv_w_down[:, :KV_LORA_RANK], w_kpe,
                             _swap_halves(w_kpe, QK_ROPE_DIM)], axis=1).astype(BF16)
    cq, ckv, k_rot = _latents(x3b, w_cat, row(q_norm_g[0]), row(kv_norm_g), cos_t, sin_t, qr=qr_rank)
    kv = _matmul(ckv, kv_w_up.reshape(KV_LORA_RANK, -1).astype(BF16))

    w_qn = q_w_up[0][:, :, :QK_NOPE_DIM].reshape(qr_rank, -1).astype(BF16)
    w_qr = q_w_up[0][:, :, QK_NOPE_DIM:].reshape(qr_rank, -1)
    qn = _matmul(cq, w_qn, scale=SOFTMAX_SCALE)
    q_rot = _rope_q(cq, w_qr.astype(BF16), _swap_halves(w_qr, QK_ROPE_DIM).astype(BF16), cos_t, sin_t,
                    scale=SOFTMAX_SCALE)
    o = _attention(qn, q_rot, kv, k_rot, batch=batch, seq=seq)
    mix = _matmul(o, attn_w_out[0].astype(BF16), wide_out=True)
    x4, x4s, lg = _ln_router(mix, x3, row(ln1_g[1]), row(ln1_b[1]), rw_hi, rw_lo)
    x5, x5b = _moe_layer(x4, x4s, lg, router_b, moe_w_in, moe_w_down, 1, row(ln2_g[1]), row(ln2_b[1]))
    x6, _ = _per_layer_input(x5b, x5, pb[1], ple_w_gate[1].astype(BF16), row(ple_b_gate[1]),
                             ple_w_proj[1].astype(BF16))
    return x6.reshape(batch, seq, d)
```

```python
import functools

import jax
import jax.numpy as jnp
from jax import lax
from jax.experimental import pallas as pl
from jax.experimental.pallas import tpu as pltpu

F32 = jnp.float32
BF16 = jnp.bfloat16

CONV_KERNEL = 31
N_HEADS = 32
QK_NOPE_DIM = 128
QK_ROPE_DIM = 64
V_HEAD_DIM = 128
KV_LORA_RANK = 512
ROPE_THETA = 10000.0
SOFTMAX_SCALE = (QK_NOPE_DIM + QK_ROPE_DIM) ** -0.5
N_EXPERTS = 32
N_GROUPS = 8
EXPERTS_PER_GROUP = N_EXPERTS // N_GROUPS
DEPTH = 2
ALPHA = (2 * DEPTH) ** 0.25
LN_EPS = 1e-5
RMS_EPS = 1e-6

LANES = 128
SUBLANES = 8
MOE_ROW_TILE = 256
WAIT_UNROLL = 8
ISSUE_UNROLL = 4
VMEM_LIMIT = 56 * 1024 * 1024


def _cparams(*sem, vmem=VMEM_LIMIT):
    return pltpu.CompilerParams(dimension_semantics=sem, vmem_limit_bytes=vmem)


def _pick(n, pref):
    t = min(n, pref)
    while n % t:
        t //= 2
    return t


def _sigmoid(x):
    return 1.0 / (1.0 + jnp.exp(-x))


def _layer_norm(r, g, b):
    mu = jnp.mean(r, axis=-1, keepdims=True)
    rc = r - mu
    var = jnp.mean(rc * rc, axis=-1, keepdims=True)
    return rc * lax.rsqrt(var + LN_EPS) * g + b


def _mm_kernel(a_ref, w_ref, b_ref, o_ref, *, scale):
    acc = jnp.dot(a_ref[...], w_ref[...], preferred_element_type=F32)
    o_ref[...] = ((acc + b_ref[...]) * scale).astype(o_ref.dtype)


def _matmul(a, w, bias=None, *, scale=1.0, wide_out=False, tm=1024, tn=1024):
    m, kd = a.shape
    n = w.shape[1]
    tm, tn = _pick(m, tm), _pick(n, tn)
    if bias is None:
        bias = jnp.zeros((1, n), F32)
    return pl.pallas_call(
        functools.partial(_mm_kernel, scale=scale),
        grid=(m // tm, n // tn),
        in_specs=[
            pl.BlockSpec((tm, kd), lambda i, j: (i, 0)),
            pl.BlockSpec((kd, tn), lambda i, j: (0, j)),
            pl.BlockSpec((1, tn), lambda i, j: (0, j)),
        ],
        out_specs=pl.BlockSpec((tm, tn), lambda i, j: (i, j)),
        out_shape=jax.ShapeDtypeStruct((m, n), F32 if wide_out else BF16),
        compiler_params=_cparams("parallel", "parallel"),
        name="matmul",
    )(a, w, bias)


def _glu_kernel(a_ref, wa_ref, wg_ref, ba_ref, bg_ref, o_ref):
    a = a_ref[...]
    val = jnp.dot(a, wa_ref[...], preferred_element_type=F32) + ba_ref[...]
    gate = jnp.dot(a, wg_ref[...], preferred_element_type=F32) + bg_ref[...]
    o_ref[...] = (val * _sigmoid(gate)).astype(o_ref.dtype)


def _glu_matmul(a, w, bias, *, tm=1024, tn=512):
    m, kd = a.shape
    n = w.shape[1] // 2
    tm, tn = _pick(m, tm), _pick(n, tn)
    nj = n // tn
    return pl.pallas_call(
        _glu_kernel,
        grid=(m // tm, nj),
        in_specs=[
            pl.BlockSpec((tm, kd), lambda i, j: (i, 0)),
            pl.BlockSpec((kd, tn), lambda i, j: (0, j)),
            pl.BlockSpec((kd, tn), lambda i, j: (0, j + nj)),
            pl.BlockSpec((1, tn), lambda i, j: (0, j)),
            pl.BlockSpec((1, tn), lambda i, j: (0, j + nj)),
        ],
        out_specs=pl.BlockSpec((tm, tn), lambda i, j: (i, j)),
        out_shape=jax.ShapeDtypeStruct((m, n), F32),
        compiler_params=_cparams("parallel", "parallel"),
        name="glu_matmul",
    )(a, w, w, bias, bias)


CONV_HALO = 32
CONV_ROWS = 64
CONV_LEAD = CONV_HALO - (CONV_KERNEL - 1)


def _conv_ln_kernel(prev_ref, cur_ref, w_ref, bdw_ref, g_ref, b_ref, o_ref, win_ref, conv_ref, *, ts, d):
    i = pl.program_id(1)

    @pl.when(i == 0)
    def _():
        win_ref[0:CONV_HALO, :] = jnp.zeros((CONV_HALO, d), F32)

    @pl.when(i > 0)
    def _():
        win_ref[0:CONV_HALO, :] = prev_ref[...]

    win_ref[CONV_HALO:CONV_HALO + ts, :] = cur_ref[...]
    win_ref[CONV_HALO + ts:, :] = jnp.zeros((SUBLANES, d), F32)

    def cols_body(cb, carry):
        cols = pl.ds(pl.multiple_of(cb * LANES, LANES), LANES)
        for r0 in range(0, ts, CONV_ROWS):
            acc = None
            for c in range(SUBLANES):
                u = None
                for a in range((CONV_LEAD + CONV_KERNEL - 1) // SUBLANES + 1):
                    k = SUBLANES * a + c - CONV_LEAD
                    if 0 <= k < CONV_KERNEL:
                        term = win_ref[pl.ds(r0 + SUBLANES * a, CONV_ROWS + SUBLANES), cols] * w_ref[k:k + 1, cols]
                        u = term if u is None else u + term
                part = u[c:c + CONV_ROWS]
                acc = part if acc is None else acc + part
            conv_ref[pl.ds(r0, CONV_ROWS), cols] = acc
        return carry

    lax.fori_loop(0, d // LANES, cols_body, 0)

    h = _layer_norm(conv_ref[...] + bdw_ref[...], g_ref[...], b_ref[...])
    o_ref[...] = (h * _sigmoid(h)).astype(o_ref.dtype)


def _conv_ln_swish(h, w_dw, b_dw, g, b, *, batch, seq, ts=256):
    t, d = h.shape
    ts = _pick(seq, ts)
    nblk = seq // ts
    hb = ts // CONV_HALO
    return pl.pallas_call(
        functools.partial(_conv_ln_kernel, ts=ts, d=d),
        grid=(batch, nblk),
        in_specs=[
            pl.BlockSpec((CONV_HALO, d), lambda bi, i: (jnp.maximum((bi * nblk + i) * hb - 1, 0), 0)),
            pl.BlockSpec((ts, d), lambda bi, i: (bi * nblk + i, 0)),
            pl.BlockSpec((CONV_KERNEL, d), lambda bi, i: (0, 0)),
            pl.BlockSpec((1, d), lambda bi, i: (0, 0)),
            pl.BlockSpec((1, d), lambda bi, i: (0, 0)),
            pl.BlockSpec((1, d), lambda bi, i: (0, 0)),
        ],
        out_specs=pl.BlockSpec((ts, d), lambda bi, i: (bi * nblk + i, 0)),
        out_shape=jax.ShapeDtypeStruct((t, d), BF16),
        scratch_shapes=[pltpu.VMEM((CONV_HALO + ts + SUBLANES, d), F32), pltpu.VMEM((ts, d), F32)],
        compiler_params=_cparams("parallel", "arbitrary"),
        name="conv_ln_swish",
    )(h, h, w_dw, b_dw, g, b)


def _slab_pitch(nslab):
    pitch = -(-nslab // SUBLANES) * SUBLANES
    return pitch if (pitch // SUBLANES) % 2 else pitch + SUBLANES


U32 = jnp.uint32
HIGH_HALF = 0xFFFF0000


def _pack_pair(lo, hi):
    lo_bits = lax.bitcast_convert_type(lo.astype(BF16).astype(F32), U32)
    hi_bits = lax.bitcast_convert_type(hi.astype(BF16).astype(F32), U32)
    return lax.shift_right_logical(lo_bits, U32(16)) | (hi_bits & U32(HIGH_HALF))


def _unpack_pair(word):
    lo = lax.bitcast_convert_type(lax.shift_left(word, U32(16)), F32)
    hi = lax.bitcast_convert_type(word & U32(HIGH_HALF), F32)
    return lo, hi


def _store_slabs(slab_ref, y_lo, y_hi, first_slab, tm, pitch):
    for j in range(y_lo.shape[1] // LANES):
        cols = slice(j * LANES, (j + 1) * LANES)
        slab_ref[pl.ds(first_slab + j, tm, stride=pitch), :] = _pack_pair(y_lo[:, cols], y_hi[:, cols])


def _zero_pad_slabs(slab_ref, nslab, tm, pitch):
    for j in range(nslab, pitch):
        slab_ref[pl.ds(j, tm, stride=pitch), :] = jnp.zeros((tm, LANES), U32)


def _load_slabs(slab_ref, base, first_slab, count, tm, pitch):
    parts = [_unpack_pair(slab_ref[pl.ds(base + first_slab + j, tm, stride=pitch), :]) for j in range(count)]
    if count == 1:
        return parts[0]
    return (jnp.concatenate([p[0] for p in parts], axis=1), jnp.concatenate([p[1] for p in parts], axis=1))


def _ln_router_kernel(mix_ref, res_ref, g_ref, b_ref, rwh_ref, rwl_ref, x_ref, slab_ref, lg_ref, *, tm, pitch):
    y = _layer_norm(ALPHA * res_ref[...] + mix_ref[...], g_ref[...], b_ref[...])
    x_ref[...] = y
    half = y.shape[1] // 2
    _store_slabs(slab_ref, y[:, :half], y[:, half:], 0, tm, pitch)
    _zero_pad_slabs(slab_ref, half // LANES, tm, pitch)
    yh = y.astype(BF16)
    yl = (y - yh.astype(F32)).astype(BF16)
    nt = (((1,), (1,)), ((), ()))
    lg = lax.dot_general(rwh_ref[...], yh, nt, preferred_element_type=F32)
    lg = lg + lax.dot_general(rwh_ref[...], yl, nt, preferred_element_type=F32)
    lg = lg + lax.dot_general(rwl_ref[...], yh, nt, preferred_element_type=F32)
    lg_ref[...] = lg


def _ln_router(mix, res, g, b, rw_hi, rw_lo, *, tm=256):
    m, d = mix.shape
    tm = _pick(m, tm)
    ne = rw_hi.shape[0]
    pitch = _slab_pitch(d // (2 * LANES))
    row = lambda i: (i, 0)
    const = lambda i: (0, 0)
    return pl.pallas_call(
        functools.partial(_ln_router_kernel, tm=tm, pitch=pitch),
        grid=(m // tm,),
        in_specs=[
            pl.BlockSpec((tm, d), row),
            pl.BlockSpec((tm, d), row),
            pl.BlockSpec((1, d), const),
            pl.BlockSpec((1, d), const),
            pl.BlockSpec((ne, d), const),
            pl.BlockSpec((ne, d), const),
        ],
        out_specs=[
            pl.BlockSpec((tm, d), row),
            pl.BlockSpec((tm * pitch, LANES), row),
            pl.BlockSpec((ne, tm), lambda i: (0, i)),
        ],
        out_shape=[
            jax.ShapeDtypeStruct((m, d), F32),
            jax.ShapeDtypeStruct((m * pitch, LANES), U32),
            jax.ShapeDtypeStruct((ne, m), F32),
        ],
        compiler_params=_cparams("parallel"),
        name="ln_router",
    )(mix, res, g, b, rw_hi, rw_lo)


def _route_kernel(lg_ref, rb_ref, e_ref, w_ref, rank_ref, cnt_ref, carry_ref, *, tt, nsteps):
    step = pl.program_id(0)

    @pl.when(step == 0)
    def _():
        carry_ref[...] = jnp.zeros_like(carry_ref)

    aff = _sigmoid(lg_ref[...])
    biased = aff + rb_ref[...]
    rows_b = [biased[e:e + 1, :] for e in range(N_EXPERTS)]
    rows_a = [aff[e:e + 1, :] for e in range(N_EXPERTS)]

    def top2_sum(v):
        m01, n01 = jnp.maximum(v[0], v[1]), jnp.minimum(v[0], v[1])
        m23, n23 = jnp.maximum(v[2], v[3]), jnp.minimum(v[2], v[3])
        return jnp.maximum(m01, m23) + jnp.maximum(jnp.minimum(m01, m23), jnp.maximum(n01, n23))

    best = top2_sum(rows_b[0:EXPERTS_PER_GROUP])
    gidx = jnp.zeros((1, tt), jnp.int32)
    for g in range(1, N_GROUPS):
        s = top2_sum(rows_b[g * EXPERTS_PER_GROUP:(g + 1) * EXPERTS_PER_GROUP])
        better = s > best
        gidx = jnp.where(better, g, gidx)
        best = jnp.where(better, s, best)

    vb = [rows_b[i] for i in range(EXPERTS_PER_GROUP)]
    va = [rows_a[i] for i in range(EXPERTS_PER_GROUP)]
    for g in range(1, N_GROUPS):
        sel = gidx == g
        for i in range(EXPERTS_PER_GROUP):
            vb[i] = jnp.where(sel, rows_b[g * EXPERTS_PER_GROUP + i], vb[i])
            va[i] = jnp.where(sel, rows_a[g * EXPERTS_PER_GROUP + i], va[i])

    i1 = jnp.zeros((1, tt), jnp.int32)
    b1, a1 = vb[0], va[0]
    for i in range(1, EXPERTS_PER_GROUP):
        better = vb[i] > b1
        i1 = jnp.where(better, i, i1)
        b1 = jnp.where(better, vb[i], b1)
        a1 = jnp.where(better, va[i], a1)
    i2 = jnp.where(i1 == 0, 1, 0).astype(jnp.int32)
    b2 = jnp.where(i1 == 0, vb[1], vb[0])
    a2 = jnp.where(i1 == 0, va[1], va[0])
    for i in range(1, EXPERTS_PER_GROUP):
        better = jnp.logical_and(vb[i] > b2, i1 != i)
        i2 = jnp.where(better, i, i2)
        b2 = jnp.where(better, vb[i], b2)
        a2 = jnp.where(better, va[i], a2)

    e1 = gidx * EXPERTS_PER_GROUP + i1
    e2 = gidx * EXPERTS_PER_GROUP + i2
    den = a1 + a2
    e_ref[0:1, :] = e1
    e_ref[1:2, :] = e2
    w_ref[0:1, :] = a1 / den
    w_ref[1:2, :] = a2 / den

    eid = lax.broadcasted_iota(jnp.int32, (N_EXPERTS, tt), 0)
    hit1 = eid == e1
    hit2 = eid == e2
    onehot = jnp.where(jnp.logical_or(hit1, hit2), 1.0, 0.0)
    src = lax.broadcasted_iota(jnp.int32, (tt, tt), 0)
    dst = lax.broadcasted_iota(jnp.int32, (tt, tt), 1)
    upper = jnp.where(src < dst, 1.0, 0.0).astype(BF16)
    before = jnp.dot(onehot.astype(BF16), upper, preferred_element_type=F32) + carry_ref[:, 0:1]
    rank_ref[0:1, :] = jnp.sum(jnp.where(hit1, before, 0.0), axis=0, keepdims=True).astype(jnp.int32)
    rank_ref[1:2, :] = jnp.sum(jnp.where(hit2, before, 0.0), axis=0, keepdims=True).astype(jnp.int32)
    carry_ref[...] += jnp.sum(onehot, axis=1, keepdims=True)

    @pl.when(step == nsteps - 1)
    def _():
        cnt_ref[...] = carry_ref[...]


def _route(logits_t, router_b, *, tt=512):
    ne, t = logits_t.shape
    tt = _pick(t, tt)
    nsteps = t // tt
    return pl.pallas_call(
        functools.partial(_route_kernel, tt=tt, nsteps=nsteps),
        grid=(nsteps,),
        in_specs=[
            pl.BlockSpec((ne, tt), lambda s: (0, s)),
            pl.BlockSpec((ne, 1), lambda s: (0, 0)),
        ],
        out_specs=[
            pl.BlockSpec((2, tt), lambda s: (0, s)),
            pl.BlockSpec((2, tt), lambda s: (0, s)),
            pl.BlockSpec((2, tt), lambda s: (0, s)),
            pl.BlockSpec((ne, LANES), lambda s: (0, 0)),
        ],
        out_shape=[
            jax.ShapeDtypeStruct((2, t), jnp.int32),
            jax.ShapeDtypeStruct((2, t), F32),
            jax.ShapeDtypeStruct((2, t), jnp.int32),
            jax.ShapeDtypeStruct((ne, LANES), F32),
        ],
        scratch_shapes=[pltpu.VMEM((ne, LANES), F32)],
        compiler_params=_cparams("arbitrary"),
        name="route",
    )(logits_t, router_b.reshape(ne, 1).astype(F32))


def _token_copy(src_ref, dst_ref, sem, src_tok, dst_tok, nslab, pitch):
    src = src_ref.at[pl.ds(pl.multiple_of(src_tok * pitch, SUBLANES), nslab)]
    dst = dst_ref.at[pl.ds(pl.multiple_of(dst_tok * pitch, SUBLANES), nslab)]
    return pltpu.make_async_copy(src, dst, sem)


WEIGHT_CAST_ROWS = 256


def _refresh_expert_weights(w_hbm, stage_ref, wres_ref, wsem, layer, expert, next_expert):
    pltpu.make_async_copy(w_hbm.at[layer, expert], stage_ref, wsem).wait()
    rows = stage_ref.shape[0]
    step = min(WEIGHT_CAST_ROWS, rows)
    for r0 in range(0, rows, step):
        wres_ref[r0:r0 + step, :] = stage_ref[r0:r0 + step, :].astype(BF16)

    @pl.when(next_expert >= 0)
    def _():
        pltpu.make_async_copy(w_hbm.at[layer, next_expert], stage_ref, wsem).start()


def _ffn_up_kernel(te_ref, tv_ref, first_ref, ne_ref, cur_ref, nxt_ref, slab_ref, w_hbm, o_ref,
                   buf_ref, sem, stage_ref, wres_ref, wsem, *, layer, nk, tm, n_tiles, nslab, pitch):
    i = pl.program_id(0)
    n_used = tv_ref[n_tiles]
    used = i < n_used
    slot = lax.rem(i, 2)
    spk = nslab // nk
    de = wres_ref.shape[1] // 2

    @pl.when(jnp.logical_and(used, i == 0))
    def _():
        pltpu.make_async_copy(w_hbm.at[layer, te_ref[0]], stage_ref, wsem).start()

    def issue(idx_ref, dst_slot):
        def body(r, carry):
            _token_copy(slab_ref, buf_ref, sem.at[dst_slot], idx_ref[0, 0, r], dst_slot * tm + r,
                        nslab, pitch).start()
            return carry

        lax.fori_loop(0, tm, body, 0, unroll=ISSUE_UNROLL)

    @pl.when(jnp.logical_and(used, i == 0))
    def _():
        issue(cur_ref, 0)

    @pl.when(i + 1 < n_used)
    def _():
        issue(nxt_ref, 1 - slot)

    @pl.when(jnp.logical_and(used, first_ref[i] > 0))
    def _():
        _refresh_expert_weights(w_hbm, stage_ref, wres_ref, wsem, layer, te_ref[i], ne_ref[i])

    @pl.when(used)
    def _():
        def drain(r, carry):
            _token_copy(slab_ref, buf_ref, sem.at[slot], 0, slot * tm + r, nslab, pitch).wait()
            return carry

        lax.fori_loop(0, tm, drain, 0, unroll=WAIT_UNROLL)
        h = None
        half = nslab * LANES
        for k in range(nk):
            x_lo, x_hi = _load_slabs(buf_ref, slot * (tm * pitch), k * spk, spk, tm, pitch)
            r0 = k * spk * LANES
            r1 = (k + 1) * spk * LANES
            part = jnp.dot(x_lo.astype(BF16), wres_ref[r0:r1, :], preferred_element_type=F32)
            part = part + jnp.dot(x_hi.astype(BF16), wres_ref[half + r0:half + r1, :], preferred_element_type=F32)
            h = part if h is None else h + part
        hg = h[:, :de]
        o_ref[...] = (hg * _sigmoid(hg) * h[:, de:]).astype(o_ref.dtype)

    @pl.when(jnp.logical_not(used))
    def _():
        o_ref[...] = jnp.zeros_like(o_ref)


def _ffn_up(slabs, src_tok, w_in, layer, plan, *, tm, tk=1024):
    d = w_in.shape[2]
    de2 = w_in.shape[3]
    nslab = d // (2 * LANES)
    pitch = _slab_pitch(nslab)
    tk = _pick(nslab * LANES, tk)
    nk = (nslab * LANES) // tk
    n_tiles = src_tok.shape[0] // tm
    idx = src_tok.reshape(n_tiles, 1, tm)

    grid_spec = pltpu.PrefetchScalarGridSpec(
        num_scalar_prefetch=4,
        grid=(n_tiles,),
        in_specs=[
            pl.BlockSpec((1, 1, tm), lambda i, *_: (i, 0, 0), memory_space=pltpu.SMEM),
            pl.BlockSpec((1, 1, tm), lambda i, *_: (jnp.minimum(i + 1, n_tiles - 1), 0, 0),
                         memory_space=pltpu.SMEM),
            pl.BlockSpec(memory_space=pl.ANY),
            pl.BlockSpec(memory_space=pl.ANY),
        ],
        out_specs=pl.BlockSpec((tm, de2 // 2), lambda i, *_: (i, 0)),
        scratch_shapes=[
            pltpu.VMEM((2 * tm * pitch, LANES), U32),
            pltpu.SemaphoreType.DMA((2,)),
            pltpu.VMEM((d, de2), F32),
            pltpu.VMEM((d, de2), BF16),
            pltpu.SemaphoreType.DMA,
        ],
    )
    return pl.pallas_call(
        functools.partial(_ffn_up_kernel, layer=layer, nk=nk, tm=tm, n_tiles=n_tiles, nslab=nslab, pitch=pitch),
        grid_spec=grid_spec,
        out_shape=jax.ShapeDtypeStruct((n_tiles * tm, de2 // 2), BF16),
        compiler_params=_cparams("arbitrary"),
        name="ffn_up",
    )(*plan, idx, idx, slabs, w_in)


def _ffn_down_kernel(te_ref, tv_ref, first_ref, ne_ref, h_ref, w_hbm, o_ref, stage_ref, wres_ref, wsem,
                     *, layer, tm, nslab, pitch, spj):
    i = pl.program_id(0)
    valid = tv_ref[i] > 0

    @pl.when(jnp.logical_and(valid, i == 0))
    def _():
        pltpu.make_async_copy(w_hbm.at[layer, te_ref[0]], stage_ref, wsem).start()

    @pl.when(jnp.logical_and(valid, first_ref[i] > 0))
    def _():
        _refresh_expert_weights(w_hbm, stage_ref, wres_ref, wsem, layer, te_ref[i], ne_ref[i])

    @pl.when(valid)
    def _():
        h = h_ref[...]
        half = nslab * LANES
        for j in range(nslab // spj):
            c0 = j * spj * LANES
            c1 = (j + 1) * spj * LANES
            y_lo = jnp.dot(h, wres_ref[:, c0:c1], preferred_element_type=F32)
            y_hi = jnp.dot(h, wres_ref[:, half + c0:half + c1], preferred_element_type=F32)
            _store_slabs(o_ref, y_lo, y_hi, j * spj, tm, pitch)
        _zero_pad_slabs(o_ref, nslab, tm, pitch)

    @pl.when(jnp.logical_not(valid))
    def _():
        o_ref[...] = jnp.zeros_like(o_ref)


def _ffn_down(h, w_down, layer, plan, *, tm, tn=1024):
    r, de = h.shape
    d = w_down.shape[3]
    nslab = d // (2 * LANES)
    pitch = _slab_pitch(nslab)
    tn = _pick(nslab * LANES, tn)
    n_tiles = r // tm

    def ii(i, tv):
        return jnp.maximum(jnp.minimum(i, tv[n_tiles] - 1), 0)

    grid_spec = pltpu.PrefetchScalarGridSpec(
        num_scalar_prefetch=4,
        grid=(n_tiles,),
        in_specs=[
            pl.BlockSpec((tm, de), lambda i, te, tv, *_: (ii(i, tv), 0)),
            pl.BlockSpec(memory_space=pl.ANY),
        ],
        out_specs=pl.BlockSpec((tm * pitch, LANES), lambda i, *_: (i, 0)),
        scratch_shapes=[
            pltpu.VMEM((de, d), F32),
            pltpu.VMEM((de, d), BF16),
            pltpu.SemaphoreType.DMA,
        ],
    )
    return pl.pallas_call(
        functools.partial(_ffn_down_kernel, layer=layer, tm=tm, nslab=nslab, pitch=pitch, spj=tn // LANES),
        grid_spec=grid_spec,
        out_shape=jax.ShapeDtypeStruct((r * pitch, LANES), U32),
        compiler_params=_cparams("arbitrary"),
        name="ffn_down",
    )(*plan, h, w_down)


def _combine_ln_kernel(cur_ref, nxt_ref, ys_ref, gw_ref, x_ref, g_ref, b_ref, xo_ref, xb_ref, buf_ref, sem,
                       *, tm, nb, nslab, pitch):
    i = pl.program_id(0)
    slot = lax.rem(i, 2)

    def copies(idx_ref, dst_slot, r):
        first = _token_copy(ys_ref, buf_ref, sem.at[dst_slot], idx_ref[0, 0, r],
                            (2 * dst_slot) * tm + r, nslab, pitch)
        second = _token_copy(ys_ref, buf_ref, sem.at[dst_slot], idx_ref[0, 0, tm + r],
                             (2 * dst_slot + 1) * tm + r, nslab, pitch)
        return first, second

    def issue(idx_ref, dst_slot):
        def body(r, carry):
            first, second = copies(idx_ref, dst_slot, r)
            first.start()
            second.start()
            return carry

        lax.fori_loop(0, tm, body, 0, unroll=ISSUE_UNROLL)

    @pl.when(i == 0)
    def _():
        issue(cur_ref, 0)

    @pl.when(i + 1 < nb)
    def _():
        issue(nxt_ref, 1 - slot)

    def drain(r, carry):
        first, second = copies(cur_ref, slot, r)
        first.wait()
        second.wait()
        return carry

    lax.fori_loop(0, tm, drain, 0, unroll=WAIT_UNROLL)
    gw = gw_ref[...]
    y0_lo, y0_hi = _load_slabs(buf_ref, (2 * slot) * (tm * pitch), 0, nslab, tm, pitch)
    y1_lo, y1_hi = _load_slabs(buf_ref, (2 * slot + 1) * (tm * pitch), 0, nslab, tm, pitch)
    y = jnp.concatenate([gw[:, 0:1] * y0_lo + gw[:, 1:2] * y1_lo,
                         gw[:, 0:1] * y0_hi + gw[:, 1:2] * y1_hi], axis=1)
    out = _layer_norm(ALPHA * x_ref[...] + y, g_ref[...], b_ref[...])
    xo_ref[...] = out
    xb_ref[...] = out.astype(BF16)


def _combine_ln(ys, pos, gate_w, x, g, b, *, tm=256):
    t, d = x.shape
    tm = _pick(t, tm)
    nb = t // tm
    nslab = d // (2 * LANES)
    pitch = _slab_pitch(nslab)
    pos_tiles = pos.reshape(2, nb, tm).transpose(1, 0, 2).reshape(nb, 1, 2 * tm)
    row = lambda i: (i, 0)
    const = lambda i: (0, 0)
    return pl.pallas_call(
        functools.partial(_combine_ln_kernel, tm=tm, nb=nb, nslab=nslab, pitch=pitch),
        grid=(nb,),
        in_specs=[
            pl.BlockSpec((1, 1, 2 * tm), lambda i: (i, 0, 0), memory_space=pltpu.SMEM),
            pl.BlockSpec((1, 1, 2 * tm), lambda i: (jnp.minimum(i + 1, nb - 1), 0, 0), memory_space=pltpu.SMEM),
            pl.BlockSpec(memory_space=pl.ANY),
            pl.BlockSpec((tm, 2), row),
            pl.BlockSpec((tm, d), row),
            pl.BlockSpec((1, d), const),
            pl.BlockSpec((1, d), const),
        ],
        out_specs=[pl.BlockSpec((tm, d), row), pl.BlockSpec((tm, d), row)],
        out_shape=[jax.ShapeDtypeStruct((t, d), F32), jax.ShapeDtypeStruct((t, d), BF16)],
        scratch_shapes=[pltpu.VMEM((4 * tm * pitch, LANES), U32), pltpu.SemaphoreType.DMA((2,))],
        compiler_params=_cparams("arbitrary"),
        name="combine_ln",
    )(pos_tiles, pos_tiles, ys, gate_w, x, g, b)


def _ple_kernel(a_ref, wg_ref, bg_ref, p_ref, wp_ref, x_ref, xo_ref, xb_ref):
    gate = _sigmoid(jnp.dot(a_ref[...], wg_ref[...], preferred_element_type=F32) + bg_ref[...])
    proj = jnp.dot(p_ref[...], wp_ref[...], preferred_element_type=F32)
    out = x_ref[...] + gate * proj
    xo_ref[...] = out
    xb_ref[...] = out.astype(BF16)


def _per_layer_input(xb, x, p, w_gate, b_gate, w_proj, *, tm=1024, tn=512):
    m, d = x.shape
    pd = p.shape[1]
    tm, tn = _pick(m, tm), _pick(d, tn)
    tile = lambda i, j: (i, j)
    return pl.pallas_call(
        _ple_kernel,
        grid=(m // tm, d // tn),
        in_specs=[
            pl.BlockSpec((tm, d), lambda i, j: (i, 0)),
            pl.BlockSpec((d, tn), lambda i, j: (0, j)),
            pl.BlockSpec((1, tn), lambda i, j: (0, j)),
            pl.BlockSpec((tm, pd), lambda i, j: (i, 0)),
            pl.BlockSpec((pd, tn), lambda i, j: (0, j)),
            pl.BlockSpec((tm, tn), tile),
        ],
        out_specs=[pl.BlockSpec((tm, tn), tile), pl.BlockSpec((tm, tn), tile)],
        out_shape=[jax.ShapeDtypeStruct((m, d), F32), jax.ShapeDtypeStruct((m, d), BF16)],
        compiler_params=_cparams("parallel", "parallel"),
        name="per_layer_input",
    )(xb, w_gate, b_gate, p, w_proj, x)


def _rope_table_kernel(pos_ref, cos_ref, sin_ref):
    half = QK_ROPE_DIM // 2
    lane = lax.broadcasted_iota(jnp.int32, (1, LANES), 1)
    freq = (lane % half).astype(F32)
    inv_freq = 1.0 / (ROPE_THETA ** (freq * (2.0 / QK_ROPE_DIM)))
    ang = pos_ref[...].astype(F32) * inv_freq
    sign = jnp.where((lane // half) % 2 == 0, -1.0, 1.0)
    cos_ref[...] = jnp.cos(ang)
    sin_ref[...] = jnp.sin(ang) * sign


def _rope_tables(positions, *, tm=2048):
    t = positions.size
    tm = _pick(t, tm)
    return pl.pallas_call(
        _rope_table_kernel,
        grid=(t // tm,),
        in_specs=[pl.BlockSpec((tm, 1), lambda i: (i, 0))],
        out_specs=[pl.BlockSpec((tm, LANES), lambda i: (i, 0))] * 2,
        out_shape=[jax.ShapeDtypeStruct((t, LANES), F32)] * 2,
        compiler_params=_cparams("parallel"),
        name="rope_tables",
    )(positions.reshape(t, 1))


def _latent_kernel(a_ref, w_ref, gq_ref, gkv_ref, cos_ref, sin_ref, cq_ref, ckv_ref, kr_ref, *, qr):
    c = jnp.dot(a_ref[...], w_ref[...], preferred_element_type=F32)
    cq = c[:, :qr]
    cq_ref[...] = (cq * lax.rsqrt(jnp.mean(cq * cq, axis=-1, keepdims=True) + RMS_EPS)
                   * gq_ref[...]).astype(BF16)
    ckv = c[:, qr:qr + KV_LORA_RANK]
    ckv_ref[...] = (ckv * lax.rsqrt(jnp.mean(ckv * ckv, axis=-1, keepdims=True) + RMS_EPS)
                    * gkv_ref[...]).astype(BF16)
    kpe = c[:, qr + KV_LORA_RANK:]
    cos = cos_ref[...]
    sin = sin_ref[...]
    rot = kpe[:, :QK_ROPE_DIM] * cos[:, :QK_ROPE_DIM] + kpe[:, QK_ROPE_DIM:] * sin[:, :QK_ROPE_DIM]
    kr_ref[...] = rot.astype(BF16)


def _latents(xb, w_cat, gq, gkv, cos_t, sin_t, *, qr, tm=512):
    m, kd = xb.shape
    n = w_cat.shape[1]
    tm = _pick(m, tm)
    row = lambda i: (i, 0)
    const = lambda i: (0, 0)
    return pl.pallas_call(
        functools.partial(_latent_kernel, qr=qr),
        grid=(m // tm,),
        in_specs=[
            pl.BlockSpec((tm, kd), row),
            pl.BlockSpec((kd, n), const),
            pl.BlockSpec((1, qr), const),
            pl.BlockSpec((1, KV_LORA_RANK), const),
            pl.BlockSpec((tm, LANES), row),
            pl.BlockSpec((tm, LANES), row),
        ],
        out_specs=[
            pl.BlockSpec((tm, qr), row),
            pl.BlockSpec((tm, KV_LORA_RANK), row),
            pl.BlockSpec((tm, QK_ROPE_DIM), row),
        ],
        out_shape=[
            jax.ShapeDtypeStruct((m, qr), BF16),
            jax.ShapeDtypeStruct((m, KV_LORA_RANK), BF16),
            jax.ShapeDtypeStruct((m, QK_ROPE_DIM), BF16),
        ],
        compiler_params=_cparams("parallel"),
        name="latents",
    )(xb, w_cat, gq, gkv, cos_t, sin_t)


def _rope_q_kernel(a_ref, w_ref, ws_ref, cos_ref, sin_ref, o_ref, *, scale, reps):
    a = a_ref[...]
    q = jnp.dot(a, w_ref[...], preferred_element_type=F32)
    qs = jnp.dot(a, ws_ref[...], preferred_element_type=F32)
    cos = jnp.tile(cos_ref[...], (1, reps))
    sin = jnp.tile(sin_ref[...], (1, reps))
    o_ref[...] = ((q * cos + qs * sin) * scale).astype(o_ref.dtype)


def _rope_q(cq, w_r, w_r_swapped, cos_t, sin_t, *, scale, tm=1024, tn=1024):
    m, kd = cq.shape
    n = w_r.shape[1]
    tm, tn = _pick(m, tm), _pick(n, tn)
    return pl.pallas_call(
        functools.partial(_rope_q_kernel, scale=scale, reps=tn // LANES),
        grid=(m // tm, n // tn),
        in_specs=[
            pl.BlockSpec((tm, kd), lambda i, j: (i, 0)),
            pl.BlockSpec((kd, tn), lambda i, j: (0, j)),
            pl.BlockSpec((kd, tn), lambda i, j: (0, j)),
            pl.BlockSpec((tm, LANES), lambda i, j: (i, 0)),
            pl.BlockSpec((tm, LANES), lambda i, j: (i, 0)),
        ],
        out_specs=pl.BlockSpec((tm, tn), lambda i, j: (i, j)),
        out_shape=jax.ShapeDtypeStruct((m, n), BF16),
        compiler_params=_cparams("parallel", "parallel"),
        name="rope_q",
    )(cq, w_r, w_r_swapped, cos_t, sin_t)


ATTN_BLOCK = 256


def _attn_kernel(qn_ref, qr_ref, kv_ref, kr_ref, o_ref, qc_ref, kc_ref, s_ref, p_ref, *, seq):
    blk = ATTN_BLOCK
    nt = (((1,), (1,)), ((), ()))
    qk = QK_NOPE_DIM + QK_ROPE_DIM
    hw = QK_NOPE_DIM + V_HEAD_DIM
    row = lax.broadcasted_iota(jnp.int32, (blk, blk), 0)
    col = lax.broadcasted_iota(jnp.int32, (blk, blk), 1)
    for h in range(2):
        qc_ref[:, :QK_NOPE_DIM] = qn_ref[:, h * QK_NOPE_DIM:(h + 1) * QK_NOPE_DIM]
        qc_ref[:, QK_NOPE_DIM:qk] = qr_ref[:, h * QK_ROPE_DIM:(h + 1) * QK_ROPE_DIM]
        kc_ref[:, :QK_NOPE_DIM] = kv_ref[:, h * hw:h * hw + QK_NOPE_DIM]
        kc_ref[:, QK_NOPE_DIM:qk] = kr_ref[...]
        for i in range(seq // blk):
            par = i % 2
            q = qc_ref[i * blk:(i + 1) * blk, :]
            m = None
            for j in range(i + 1):
                s = lax.dot_general(q, kc_ref[j * blk:(j + 1) * blk, :], nt, preferred_element_type=F32)
                if j == i:
                    s = jnp.where(col <= row, s, -jnp.inf)
                s_ref[par, :, j * blk:(j + 1) * blk] = s
                mj = jnp.max(s, axis=-1, keepdims=True)
                m = mj if m is None else jnp.maximum(m, mj)
            l = None
            for j in range(i + 1):
                p = jnp.exp(s_ref[par, :, j * blk:(j + 1) * blk] - m)
                lj = jnp.sum(p, axis=-1, keepdims=True)
                l = lj if l is None else l + lj
                p_ref[par, :, j * blk:(j + 1) * blk] = p.astype(BF16)
            w = (i + 1) * blk
            v = kv_ref[0:w, h * hw + QK_NOPE_DIM:(h + 1) * hw]
            acc = jnp.dot(p_ref[par, :, 0:w], v, preferred_element_type=F32)
            o_ref[i * blk:(i + 1) * blk, h * V_HEAD_DIM:(h + 1) * V_HEAD_DIM] = (acc / l).astype(o_ref.dtype)


def _attention(qn, qr, kv, kr, *, batch, seq):
    t = qn.shape[0]
    hp = N_HEADS // 2
    kvw = 2 * (QK_NOPE_DIM + V_HEAD_DIM)
    qk = QK_NOPE_DIM + QK_ROPE_DIM
    return pl.pallas_call(
        functools.partial(_attn_kernel, seq=seq),
        grid=(batch, hp),
        in_specs=[
            pl.BlockSpec((seq, 2 * QK_NOPE_DIM), lambda b, g: (b, g)),
            pl.BlockSpec((seq, 2 * QK_ROPE_DIM), lambda b, g: (b, g)),
            pl.BlockSpec((seq, kvw), lambda b, g: (b, g)),
            pl.BlockSpec((seq, QK_ROPE_DIM), lambda b, g: (b, 0)),
        ],
        out_specs=pl.BlockSpec((seq, 2 * V_HEAD_DIM), lambda b, g: (b, g)),
        out_shape=jax.ShapeDtypeStruct((t, N_HEADS * V_HEAD_DIM), BF16),
        scratch_shapes=[
            pltpu.VMEM((seq, qk), BF16),
            pltpu.VMEM((seq, qk), BF16),
            pltpu.VMEM((2, ATTN_BLOCK, seq), F32),
            pltpu.VMEM((2, ATTN_BLOCK, seq), BF16),
        ],
        compiler_params=_cparams("parallel", "parallel"),
        name="attention",
    )(qn, qr, kv, kr)


def _split_hi_lo(w):
    hi = w.astype(BF16)
    lo = (w - hi.astype(F32)).astype(BF16)
    return hi, lo


def _swap_halves(w, width):
    k, n = w.shape
    return w.reshape(k, n // width, 2, width // 2)[:, :, ::-1, :].reshape(k, n)


def _moe_layer(x, x_slabs, logits_t, router_b, w_in, w_down, layer, g, b):
    t, d = x.shape
    tm = MOE_ROW_TILE
    experts, gate_w, rank, counts = _route(logits_t, router_b)

    cnt = counts[:, 0].astype(jnp.int32)
    tiles_per_expert = (cnt + tm - 1) // tm
    eid = jnp.arange(N_EXPERTS, dtype=jnp.int32)
    tile_end = jnp.sum(jnp.where(eid[None, :] <= eid[:, None], tiles_per_expert[None, :], 0), axis=1)
    row_off = (tile_end - tiles_per_expert) * tm
    n_tiles = (2 * t) // tm + N_EXPERTS
    n_rows = n_tiles * tm
    tile_id = jnp.arange(n_tiles, dtype=jnp.int32)
    n_used = tile_end[-1].astype(jnp.int32)
    tile_valid = jnp.concatenate([(tile_id < n_used).astype(jnp.int32), n_used[None]])
    last_tile = jnp.maximum(n_used - 1, 0)
    before = tile_end[None, :] <= jnp.minimum(tile_id, last_tile)[:, None]
    tile_expert = jnp.minimum(jnp.sum(before.astype(jnp.int32), axis=1), N_EXPERTS - 1)
    prev_expert = jnp.concatenate([jnp.full((1,), -1, jnp.int32), tile_expert[:-1]])
    run_first = jnp.logical_and(tile_id < n_used, tile_expert != prev_expert).astype(jnp.int32)
    later = jnp.logical_and(eid[None, :] > eid[:, None], tiles_per_expert[None, :] > 0)
    next_nonempty = jnp.min(jnp.where(later, eid[None, :], N_EXPERTS), axis=1)
    next_nonempty = jnp.where(next_nonempty < N_EXPERTS, next_nonempty, -1)
    run_next = jnp.sum(jnp.where(tile_expert[:, None] == eid, next_nonempty, 0), axis=1).astype(jnp.int32)
    plan = (tile_expert, tile_valid, run_first, run_next)
    onehot = experts[:, :, None] == eid
    pos = jnp.sum(jnp.where(onehot, row_off, 0), axis=-1) + rank
    tok = jnp.broadcast_to(jnp.arange(t, dtype=jnp.int32)[None, :], (2, t))
    src_tok = jnp.zeros((n_rows,), jnp.int32).at[pos.reshape(-1)].set(tok.reshape(-1))

    hmid = _ffn_up(x_slabs, src_tok, w_in, layer, plan, tm=tm)
    ys = _ffn_down(hmid, w_down, layer, plan, tm=tm)
    return _combine_ln(ys, pos, gate_w.T, x, g, b)


def kernel(x, p, positions, conv_w_in, conv_b_in, conv_w_dw, conv_b_dw, conv_ln_g, conv_ln_b, conv_w_out, conv_b_out, kv_w_down, kv_norm_g, kv_w_up, q_w_down, q_norm_g, q_w_up, attn_w_out, router_w, router_b, moe_w_in, moe_w_down, ln1_g, ln1_b, ln2_g, ln2_b, ple_w_gate, ple_b_gate, ple_w_proj):
    batch, seq, d = x.shape
    t = batch * seq
    row = lambda v: v.reshape(1, -1).astype(F32)

    x0 = x.reshape(t, d)
    rw_hi, rw_lo = _split_hi_lo(router_w.T)
    pb = p.reshape(DEPTH, t, -1).astype(BF16)

    glu = _glu_matmul(x0.astype(BF16), conv_w_in[0].astype(BF16), row(conv_b_in[0]))
    hn = _conv_ln_swish(glu, conv_w_dw[0].reshape(CONV_KERNEL, d), row(conv_b_dw[0]),
                        row(conv_ln_g[0]), row(conv_ln_b[0]), batch=batch, seq=seq)
    mix = _matmul(hn, conv_w_out[0].astype(BF16), row(conv_b_out[0]), wide_out=True)
    x1, x1s, lg = _ln_router(mix, x0, row(ln1_g[0]), row(ln1_b[0]), rw_hi, rw_lo)
    x2, x2b = _moe_layer(x1, x1s, lg, router_b, moe_w_in, moe_w_down, 0, row(ln2_g[0]), row(ln2_b[0]))
    x3, x3b = _per_layer_input(x2b, x2, pb[0], ple_w_gate[0].astype(BF16), row(ple_b_gate[0]),
                               ple_w_proj[0].astype(BF16))

    cos_t, sin_t = _rope_tables(positions)
    qr_rank = q_w_down.shape[2]
    w_kpe = kv_w_down[:, KV_LORA_RANK:]
    w_cat = jnp.concatenate([q_w_down[0], kv_w_down[:, :KV_LORA_RANK], w_kpe,
                             _swap_halves(w_kpe, QK_ROPE_DIM)], axis=1).astype(BF16)
    cq, ckv, k_rot = _latents(x3b, w_cat, row(q_norm_g[0]), row(kv_norm_g), cos_t, sin_t, qr=qr_rank)
    kv = _matmul(ckv, kv_w_up.reshape(KV_LORA_RANK, -1).astype(BF16))

    w_qn = q_w_up[0][:, :, :QK_NOPE_DIM].reshape(qr_rank, -1).astype(BF16)
    w_qr = q_w_up[0][:, :, QK_NOPE_DIM:].reshape(qr_rank, -1)
    qn = _matmul(cq, w_qn, scale=SOFTMAX_SCALE)
    q_rot = _rope_q(cq, w_qr.astype(BF16), _swap_halves(w_qr, QK_ROPE_DIM).astype(BF16), cos_t, sin_t,
                    scale=SOFTMAX_SCALE)
    o = _attention(qn, q_rot, kv, k_rot, batch=batch, seq=seq)
    mix = _matmul(o, attn_w_out[0].astype(BF16), wide_out=True)
    x4, x4s, lg = _ln_router(mix, x3, row(ln1_g[1]), row(ln1_b[1]), rw_hi, rw_lo)
    x5, x5b = _moe_layer(x4, x4s, lg, router_b, moe_w_in, moe_w_down, 1, row(ln2_g[1]), row(ln2_b[1]))
    x6, _ = _per_layer_input(x5b, x5, pb[1], ple_w_gate[1].astype(BF16), row(ple_b_gate[1]),
                             ple_w_proj[1].astype(BF16))
    return x6.reshape(batch, seq, d)
```

```python
import functools

import jax
import jax.numpy as jnp
from jax import lax
from jax.experimental import pallas as pl
from jax.experimental.pallas import tpu as pltpu

F32 = jnp.float32
BF16 = jnp.bfloat16

CONV_KERNEL = 31
N_HEADS = 32
QK_NOPE_DIM = 128
QK_ROPE_DIM = 64
V_HEAD_DIM = 128
KV_LORA_RANK = 512
ROPE_THETA = 10000.0
SOFTMAX_SCALE = (QK_NOPE_DIM + QK_ROPE_DIM) ** -0.5
N_EXPERTS = 32
N_GROUPS = 8
EXPERTS_PER_GROUP = N_EXPERTS // N_GROUPS
DEPTH = 2
ALPHA = (2 * DEPTH) ** 0.25
LN_EPS = 1e-5
RMS_EPS = 1e-6

LANES = 128
SUBLANES = 8
MOE_ROW_TILE = 256
WAIT_UNROLL = 8
ISSUE_UNROLL = 8
VMEM_LIMIT = 56 * 1024 * 1024


def _cparams(*sem, vmem=VMEM_LIMIT):
    return pltpu.CompilerParams(dimension_semantics=sem, vmem_limit_bytes=vmem)


def _pick(n, pref):
    t = min(n, pref)
    while n % t:
        t //= 2
    return t


def _sigmoid(x):
    return 1.0 / (1.0 + jnp.exp(-x))


def _layer_norm(r, g, b):
    mu = jnp.mean(r, axis=-1, keepdims=True)
    rc = r - mu
    var = jnp.mean(rc * rc, axis=-1, keepdims=True)
    return rc * lax.rsqrt(var + LN_EPS) * g + b


def _mm_kernel(a_ref, w_ref, b_ref, o_ref, *, scale):
    acc = jnp.dot(a_ref[...], w_ref[...], preferred_element_type=F32)
    o_ref[...] = ((acc + b_ref[...]) * scale).astype(o_ref.dtype)


def _matmul(a, w, bias=None, *, scale=1.0, tm=1024, tn=1024):
    m, kd = a.shape
    n = w.shape[1]
    tm, tn = _pick(m, tm), _pick(n, tn)
    if bias is None:
        bias = jnp.zeros((1, n), F32)
    return pl.pallas_call(
        functools.partial(_mm_kernel, scale=scale),
        grid=(m // tm, n // tn),
        in_specs=[
            pl.BlockSpec((tm, kd), lambda i, j: (i, 0)),
            pl.BlockSpec((kd, tn), lambda i, j: (0, j)),
            pl.BlockSpec((1, tn), lambda i, j: (0, j)),
        ],
        out_specs=pl.BlockSpec((tm, tn), lambda i, j: (i, j)),
        out_shape=jax.ShapeDtypeStruct((m, n), BF16),
        compiler_params=_cparams("parallel", "parallel"),
        name="matmul",
    )(a, w, bias)


def _glu_kernel(a_ref, wa_ref, wg_ref, ba_ref, bg_ref, o_ref):
    a = a_ref[...]
    val = jnp.dot(a, wa_ref[...], preferred_element_type=F32) + ba_ref[...]
    gate = jnp.dot(a, wg_ref[...], preferred_element_type=F32) + bg_ref[...]
    o_ref[...] = (val * _sigmoid(gate)).astype(o_ref.dtype)


def _glu_matmul(a, w, bias, *, tm=1024, tn=512):
    m, kd = a.shape
    n = w.shape[1] // 2
    tm, tn = _pick(m, tm), _pick(n, tn)
    nj = n // tn
    return pl.pallas_call(
        _glu_kernel,
        grid=(m // tm, nj),
        in_specs=[
            pl.BlockSpec((tm, kd), lambda i, j: (i, 0)),
            pl.BlockSpec((kd, tn), lambda i, j: (0, j)),
            pl.BlockSpec((kd, tn), lambda i, j: (0, j + nj)),
            pl.BlockSpec((1, tn), lambda i, j: (0, j)),
            pl.BlockSpec((1, tn), lambda i, j: (0, j + nj)),
        ],
        out_specs=pl.BlockSpec((tm, tn), lambda i, j: (i, j)),
        out_shape=jax.ShapeDtypeStruct((m, n), F32),
        compiler_params=_cparams("parallel", "parallel"),
        name="glu_matmul",
    )(a, w, w, bias, bias)


CONV_HALO = 32
CONV_ROWS = 64
CONV_LEAD = CONV_HALO - (CONV_KERNEL - 1)


def _conv_ln_kernel(prev_ref, cur_ref, w_ref, bdw_ref, g_ref, b_ref, o_ref, win_ref, conv_ref, *, ts, d):
    i = pl.program_id(1)

    @pl.when(i == 0)
    def _():
        win_ref[0:CONV_HALO, :] = jnp.zeros((CONV_HALO, d), F32)

    @pl.when(i > 0)
    def _():
        win_ref[0:CONV_HALO, :] = prev_ref[...]

    win_ref[CONV_HALO:CONV_HALO + ts, :] = cur_ref[...]
    win_ref[CONV_HALO + ts:, :] = jnp.zeros((SUBLANES, d), F32)

    def cols_body(cb, carry):
        cols = pl.ds(pl.multiple_of(cb * LANES, LANES), LANES)
        for r0 in range(0, ts, CONV_ROWS):
            acc = None
            for c in range(SUBLANES):
                u = None
                for a in range((CONV_LEAD + CONV_KERNEL - 1) // SUBLANES + 1):
                    k = SUBLANES * a + c - CONV_LEAD
                    if 0 <= k < CONV_KERNEL:
                        term = win_ref[pl.ds(r0 + SUBLANES * a, CONV_ROWS + SUBLANES), cols] * w_ref[k:k + 1, cols]
                        u = term if u is None else u + term
                part = u[c:c + CONV_ROWS]
                acc = part if acc is None else acc + part
            conv_ref[pl.ds(r0, CONV_ROWS), cols] = acc
        return carry

    lax.fori_loop(0, d // LANES, cols_body, 0)

    h = _layer_norm(conv_ref[...] + bdw_ref[...], g_ref[...], b_ref[...])
    o_ref[...] = (h * _sigmoid(h)).astype(o_ref.dtype)


def _conv_ln_swish(h, w_dw, b_dw, g, b, *, batch, seq, ts=256):
    t, d = h.shape
    ts = _pick(seq, ts)
    nblk = seq // ts
    hb = ts // CONV_HALO
    return pl.pallas_call(
        functools.partial(_conv_ln_kernel, ts=ts, d=d),
        grid=(batch, nblk),
        in_specs=[
            pl.BlockSpec((CONV_HALO, d), lambda bi, i: (jnp.maximum((bi * nblk + i) * hb - 1, 0), 0)),
            pl.BlockSpec((ts, d), lambda bi, i: (bi * nblk + i, 0)),
            pl.BlockSpec((CONV_KERNEL, d), lambda bi, i: (0, 0)),
            pl.BlockSpec((1, d), lambda bi, i: (0, 0)),
            pl.BlockSpec((1, d), lambda bi, i: (0, 0)),
            pl.BlockSpec((1, d), lambda bi, i: (0, 0)),
        ],
        out_specs=pl.BlockSpec((ts, d), lambda bi, i: (bi * nblk + i, 0)),
        out_shape=jax.ShapeDtypeStruct((t, d), BF16),
        scratch_shapes=[pltpu.VMEM((CONV_HALO + ts + SUBLANES, d), F32), pltpu.VMEM((ts, d), F32)],
        compiler_params=_cparams("parallel", "arbitrary"),
        name="conv_ln_swish",
    )(h, h, w_dw, b_dw, g, b)


def _slab_pitch(nslab):
    pitch = -(-nslab // SUBLANES) * SUBLANES
    return pitch if (pitch // SUBLANES) % 2 else pitch + SUBLANES


U32 = jnp.uint32
HIGH_HALF = 0xFFFF0000


def _pack_pair(lo, hi):
    lo_bits = lax.bitcast_convert_type(lo.astype(BF16).astype(F32), U32)
    hi_bits = lax.bitcast_convert_type(hi.astype(BF16).astype(F32), U32)
    return lax.shift_right_logical(lo_bits, U32(16)) | (hi_bits & U32(HIGH_HALF))


def _unpack_pair(word):
    lo = lax.bitcast_convert_type(lax.shift_left(word, U32(16)), F32)
    hi = lax.bitcast_convert_type(word & U32(HIGH_HALF), F32)
    return lo, hi


def _store_slabs(slab_ref, y_lo, y_hi, first_slab, tm, pitch):
    for j in range(y_lo.shape[1] // LANES):
        cols = slice(j * LANES, (j + 1) * LANES)
        slab_ref[pl.ds(first_slab + j, tm, stride=pitch), :] = _pack_pair(y_lo[:, cols], y_hi[:, cols])


def _zero_pad_slabs(slab_ref, nslab, tm, pitch):
    for j in range(nslab, pitch):
        slab_ref[pl.ds(j, tm, stride=pitch), :] = jnp.zeros((tm, LANES), U32)


def _load_slabs(slab_ref, base, first_slab, count, tm, pitch):
    parts = [_unpack_pair(slab_ref[pl.ds(base + first_slab + j, tm, stride=pitch), :]) for j in range(count)]
    if count == 1:
        return parts[0]
    return (jnp.concatenate([p[0] for p in parts], axis=1), jnp.concatenate([p[1] for p in parts], axis=1))


def _ln_router_kernel(mix_ref, res_ref, g_ref, b_ref, rwh_ref, rwl_ref, x_ref, slab_ref, lg_ref, *, tm, pitch):
    y = _layer_norm(ALPHA * res_ref[...] + mix_ref[...].astype(F32), g_ref[...], b_ref[...])
    x_ref[...] = y
    half = y.shape[1] // 2
    _store_slabs(slab_ref, y[:, :half], y[:, half:], 0, tm, pitch)
    _zero_pad_slabs(slab_ref, half // LANES, tm, pitch)
    yh = y.astype(BF16)
    yl = (y - yh.astype(F32)).astype(BF16)
    nt = (((1,), (1,)), ((), ()))
    lg = lax.dot_general(rwh_ref[...], yh, nt, preferred_element_type=F32)
    lg = lg + lax.dot_general(rwh_ref[...], yl, nt, preferred_element_type=F32)
    lg = lg + lax.dot_general(rwl_ref[...], yh, nt, preferred_element_type=F32)
    lg_ref[...] = lg


def _ln_router(mix, res, g, b, rw_hi, rw_lo, *, tm=256):
    m, d = mix.shape
    tm = _pick(m, tm)
    ne = rw_hi.shape[0]
    pitch = _slab_pitch(d // (2 * LANES))
    row = lambda i: (i, 0)
    const = lambda i: (0, 0)
    return pl.pallas_call(
        functools.partial(_ln_router_kernel, tm=tm, pitch=pitch),
        grid=(m // tm,),
        in_specs=[
            pl.BlockSpec((tm, d), row),
            pl.BlockSpec((tm, d), row),
            pl.BlockSpec((1, d), const),
            pl.BlockSpec((1, d), const),
            pl.BlockSpec((ne, d), const),
            pl.BlockSpec((ne, d), const),
        ],
        out_specs=[
            pl.BlockSpec((tm, d), row),
            pl.BlockSpec((tm * pitch, LANES), row),
            pl.BlockSpec((ne, tm), lambda i: (0, i)),
        ],
        out_shape=[
            jax.ShapeDtypeStruct((m, d), F32),
            jax.ShapeDtypeStruct((m * pitch, LANES), U32),
            jax.ShapeDtypeStruct((ne, m), F32),
        ],
        compiler_params=_cparams("parallel"),
        name="ln_router",
    )(mix, res, g, b, rw_hi, rw_lo)


def _route_kernel(lg_ref, rb_ref, e_ref, w_ref, rank_ref, cnt_ref, carry_ref, *, tt, nsteps):
    step = pl.program_id(0)

    @pl.when(step == 0)
    def _():
        carry_ref[...] = jnp.zeros_like(carry_ref)

    aff = _sigmoid(lg_ref[...])
    biased = aff + rb_ref[...]
    rows_b = [biased[e:e + 1, :] for e in range(N_EXPERTS)]
    rows_a = [aff[e:e + 1, :] for e in range(N_EXPERTS)]

    def top2_sum(v):
        m01, n01 = jnp.maximum(v[0], v[1]), jnp.minimum(v[0], v[1])
        m23, n23 = jnp.maximum(v[2], v[3]), jnp.minimum(v[2], v[3])
        return jnp.maximum(m01, m23) + jnp.maximum(jnp.minimum(m01, m23), jnp.maximum(n01, n23))

    best = top2_sum(rows_b[0:EXPERTS_PER_GROUP])
    gidx = jnp.zeros((1, tt), jnp.int32)
    for g in range(1, N_GROUPS):
        s = top2_sum(rows_b[g * EXPERTS_PER_GROUP:(g + 1) * EXPERTS_PER_GROUP])
        better = s > best
        gidx = jnp.where(better, g, gidx)
        best = jnp.where(better, s, best)

    vb = [rows_b[i] for i in range(EXPERTS_PER_GROUP)]
    va = [rows_a[i] for i in range(EXPERTS_PER_GROUP)]
    for g in range(1, N_GROUPS):
        sel = gidx == g
        for i in range(EXPERTS_PER_GROUP):
            vb[i] = jnp.where(sel, rows_b[g * EXPERTS_PER_GROUP + i], vb[i])
            va[i] = jnp.where(sel, rows_a[g * EXPERTS_PER_GROUP + i], va[i])

    i1 = jnp.zeros((1, tt), jnp.int32)
    b1, a1 = vb[0], va[0]
    for i in range(1, EXPERTS_PER_GROUP):
        better = vb[i] > b1
        i1 = jnp.where(better, i, i1)
        b1 = jnp.where(better, vb[i], b1)
        a1 = jnp.where(better, va[i], a1)
    i2 = jnp.where(i1 == 0, 1, 0).astype(jnp.int32)
    b2 = jnp.where(i1 == 0, vb[1], vb[0])
    a2 = jnp.where(i1 == 0, va[1], va[0])
    for i in range(1, EXPERTS_PER_GROUP):
        better = jnp.logical_and(vb[i] > b2, i1 != i)
        i2 = jnp.where(better, i, i2)
        b2 = jnp.where(better, vb[i], b2)
        a2 = jnp.where(better, va[i], a2)

    e1 = gidx * EXPERTS_PER_GROUP + i1
    e2 = gidx * EXPERTS_PER_GROUP + i2
    den = a1 + a2
    e_ref[0:1, :] = e1
    e_ref[1:2, :] = e2
    w_ref[0:1, :] = a1 / den
    w_ref[1:2, :] = a2 / den

    eid = lax.broadcasted_iota(jnp.int32, (N_EXPERTS, tt), 0)
    hit1 = eid == e1
    hit2 = eid == e2
    onehot = jnp.where(jnp.logical_or(hit1, hit2), 1.0, 0.0)
    src = lax.broadcasted_iota(jnp.int32, (tt, tt), 0)
    dst = lax.broadcasted_iota(jnp.int32, (tt, tt), 1)
    upper = jnp.where(src < dst, 1.0, 0.0).astype(BF16)
    before = jnp.dot(onehot.astype(BF16), upper, preferred_element_type=F32) + carry_ref[:, 0:1]
    rank_ref[0:1, :] = jnp.sum(jnp.where(hit1, before, 0.0), axis=0, keepdims=True).astype(jnp.int32)
    rank_ref[1:2, :] = jnp.sum(jnp.where(hit2, before, 0.0), axis=0, keepdims=True).astype(jnp.int32)
    carry_ref[...] += jnp.sum(onehot, axis=1, keepdims=True)

    @pl.when(step == nsteps - 1)
    def _():
        cnt_ref[...] = carry_ref[...]


def _route(logits_t, router_b, *, tt=512):
    ne, t = logits_t.shape
    tt = _pick(t, tt)
    nsteps = t // tt
    return pl.pallas_call(
        functools.partial(_route_kernel, tt=tt, nsteps=nsteps),
        grid=(nsteps,),
        in_specs=[
            pl.BlockSpec((ne, tt), lambda s: (0, s)),
            pl.BlockSpec((ne, 1), lambda s: (0, 0)),
        ],
        out_specs=[
            pl.BlockSpec((2, tt), lambda s: (0, s)),
            pl.BlockSpec((2, tt), lambda s: (0, s)),
            pl.BlockSpec((2, tt), lambda s: (0, s)),
            pl.BlockSpec((ne, LANES), lambda s: (0, 0)),
        ],
        out_shape=[
            jax.ShapeDtypeStruct((2, t), jnp.int32),
            jax.ShapeDtypeStruct((2, t), F32),
            jax.ShapeDtypeStruct((2, t), jnp.int32),
            jax.ShapeDtypeStruct((ne, LANES), F32),
        ],
        scratch_shapes=[pltpu.VMEM((ne, LANES), F32)],
        compiler_params=_cparams("arbitrary"),
        name="route",
    )(logits_t, router_b.reshape(ne, 1).astype(F32))


def _token_copy(src_ref, dst_ref, sem, src_tok, dst_tok, nslab, pitch):
    src = src_ref.at[pl.ds(pl.multiple_of(src_tok * pitch, SUBLANES), nslab)]
    dst = dst_ref.at[pl.ds(pl.multiple_of(dst_tok * pitch, SUBLANES), nslab)]
    return pltpu.make_async_copy(src, dst, sem)


WEIGHT_CAST_ROWS = 256


def _refresh_expert_weights(w_hbm, stage_ref, wres_ref, wsem, layer, expert, next_expert):
    pltpu.make_async_copy(w_hbm.at[layer, expert], stage_ref, wsem).wait()
    rows = stage_ref.shape[0]
    step = min(WEIGHT_CAST_ROWS, rows)
    for r0 in range(0, rows, step):
        wres_ref[r0:r0 + step, :] = stage_ref[r0:r0 + step, :].astype(BF16)

    @pl.when(next_expert >= 0)
    def _():
        pltpu.make_async_copy(w_hbm.at[layer, next_expert], stage_ref, wsem).start()


def _ffn_up_kernel(te_ref, tv_ref, first_ref, ne_ref, cur_ref, nxt_ref, slab_ref, w_hbm, o_ref,
                   buf_ref, sem, stage_ref, wres_ref, wsem, *, layer, nk, tm, n_tiles, nslab, pitch):
    i = pl.program_id(0)
    n_used = tv_ref[n_tiles]
    used = i < n_used
    slot = lax.rem(i, 2)
    spk = nslab // nk
    de = wres_ref.shape[1] // 2

    @pl.when(jnp.logical_and(used, i == 0))
    def _():
        pltpu.make_async_copy(w_hbm.at[layer, te_ref[0]], stage_ref, wsem).start()

    def issue(idx_ref, dst_slot):
        def body(r, carry):
            _token_copy(slab_ref, buf_ref, sem.at[dst_slot], idx_ref[0, 0, r], dst_slot * tm + r,
                        nslab, pitch).start()
            return carry

        lax.fori_loop(0, tm, body, 0, unroll=ISSUE_UNROLL)

    @pl.when(jnp.logical_and(used, i == 0))
    def _():
        issue(cur_ref, 0)

    @pl.when(i + 1 < n_used)
    def _():
        issue(nxt_ref, 1 - slot)

    @pl.when(jnp.logical_and(used, first_ref[i] > 0))
    def _():
        _refresh_expert_weights(w_hbm, stage_ref, wres_ref, wsem, layer, te_ref[i], ne_ref[i])

    @pl.when(used)
    def _():
        def drain(r, carry):
            _token_copy(slab_ref, buf_ref, sem.at[slot], 0, slot * tm + r, nslab, pitch).wait()
            return carry

        lax.fori_loop(0, tm, drain, 0, unroll=WAIT_UNROLL)
        h = None
        half = nslab * LANES
        for k in range(nk):
            x_lo, x_hi = _load_slabs(buf_ref, slot * (tm * pitch), k * spk, spk, tm, pitch)
            r0 = k * spk * LANES
            r1 = (k + 1) * spk * LANES
            part = jnp.dot(x_lo.astype(BF16), wres_ref[r0:r1, :], preferred_element_type=F32)
            part = part + jnp.dot(x_hi.astype(BF16), wres_ref[half + r0:half + r1, :], preferred_element_type=F32)
            h = part if h is None else h + part
        hg = h[:, :de]
        o_ref[...] = (hg * _sigmoid(hg) * h[:, de:]).astype(o_ref.dtype)

    @pl.when(jnp.logical_not(used))
    def _():
        o_ref[...] = jnp.zeros_like(o_ref)


def _ffn_up(slabs, src_tok, w_in, layer, plan, *, tm, tk=1024):
    d = w_in.shape[2]
    de2 = w_in.shape[3]
    nslab = d // (2 * LANES)
    pitch = _slab_pitch(nslab)
    tk = _pick(nslab * LANES, tk)
    nk = (nslab * LANES) // tk
    n_tiles = src_tok.shape[0] // tm
    idx = src_tok.reshape(n_tiles, 1, tm)

    grid_spec = pltpu.PrefetchScalarGridSpec(
        num_scalar_prefetch=4,
        grid=(n_tiles,),
        in_specs=[
            pl.BlockSpec((1, 1, tm), lambda i, *_: (i, 0, 0), memory_space=pltpu.SMEM),
            pl.BlockSpec((1, 1, tm), lambda i, *_: (jnp.minimum(i + 1, n_tiles - 1), 0, 0),
                         memory_space=pltpu.SMEM),
            pl.BlockSpec(memory_space=pl.ANY),
            pl.BlockSpec(memory_space=pl.ANY),
        ],
        out_specs=pl.BlockSpec((tm, de2 // 2), lambda i, *_: (i, 0)),
        scratch_shapes=[
            pltpu.VMEM((2 * tm * pitch, LANES), U32),
            pltpu.SemaphoreType.DMA((2,)),
            pltpu.VMEM((d, de2), F32),
            pltpu.VMEM((d, de2), BF16),
            pltpu.SemaphoreType.DMA,
        ],
    )
    return pl.pallas_call(
        functools.partial(_ffn_up_kernel, layer=layer, nk=nk, tm=tm, n_tiles=n_tiles, nslab=nslab, pitch=pitch),
        grid_spec=grid_spec,
        out_shape=jax.ShapeDtypeStruct((n_tiles * tm, de2 // 2), BF16),
        compiler_params=_cparams("arbitrary"),
        name="ffn_up",
    )(*plan, idx, idx, slabs, w_in)


def _ffn_down_kernel(te_ref, tv_ref, first_ref, ne_ref, h_ref, w_hbm, o_ref, stage_ref, wres_ref, wsem,
                     *, layer, tm, nslab, pitch, spj):
    i = pl.program_id(0)
    valid = tv_ref[i] > 0

    @pl.when(jnp.logical_and(valid, i == 0))
    def _():
        pltpu.make_async_copy(w_hbm.at[layer, te_ref[0]], stage_ref, wsem).start()

    @pl.when(jnp.logical_and(valid, first_ref[i] > 0))
    def _():
        _refresh_expert_weights(w_hbm, stage_ref, wres_ref, wsem, layer, te_ref[i], ne_ref[i])

    @pl.when(valid)
    def _():
        h = h_ref[...]
        half = nslab * LANES
        for j in range(nslab // spj):
            c0 = j * spj * LANES
            c1 = (j + 1) * spj * LANES
            y_lo = jnp.dot(h, wres_ref[:, c0:c1], preferred_element_type=F32)
            y_hi = jnp.dot(h, wres_ref[:, half + c0:half + c1], preferred_element_type=F32)
            _store_slabs(o_ref, y_lo, y_hi, j * spj, tm, pitch)
        _zero_pad_slabs(o_ref, nslab, tm, pitch)

    @pl.when(jnp.logical_not(valid))
    def _():
        o_ref[...] = jnp.zeros_like(o_ref)


def _ffn_down(h, w_down, layer, plan, *, tm, tn=1024):
    r, de = h.shape
    d = w_down.shape[3]
    nslab = d // (2 * LANES)
    pitch = _slab_pitch(nslab)
    tn = _pick(nslab * LANES, tn)
    n_tiles = r // tm

    def ii(i, tv):
        return jnp.maximum(jnp.minimum(i, tv[n_tiles] - 1), 0)

    grid_spec = pltpu.PrefetchScalarGridSpec(
        num_scalar_prefetch=4,
        grid=(n_tiles,),
        in_specs=[
            pl.BlockSpec((tm, de), lambda i, te, tv, *_: (ii(i, tv), 0)),
            pl.BlockSpec(memory_space=pl.ANY),
        ],
        out_specs=pl.BlockSpec((tm * pitch, LANES), lambda i, *_: (i, 0)),
        scratch_shapes=[
            pltpu.VMEM((de, d), F32),
            pltpu.VMEM((de, d), BF16),
            pltpu.SemaphoreType.DMA,
        ],
    )
    return pl.pallas_call(
        functools.partial(_ffn_down_kernel, layer=layer, tm=tm, nslab=nslab, pitch=pitch, spj=tn // LANES),
        grid_spec=grid_spec,
        out_shape=jax.ShapeDtypeStruct((r * pitch, LANES), U32),
        compiler_params=_cparams("arbitrary"),
        name="ffn_down",
    )(*plan, h, w_down)


def _combine_ln_kernel(cur_ref, nxt_ref, ys_ref, gw_ref, x_ref, g_ref, b_ref, xo_ref, xb_ref, buf_ref, sem,
                       *, tm, nb, nslab, pitch):
    i = pl.program_id(0)
    slot = lax.rem(i, 2)

    def copies(idx_ref, dst_slot, r):
        first = _token_copy(ys_ref, buf_ref, sem.at[dst_slot], idx_ref[0, 0, r],
                            (2 * dst_slot) * tm + r, nslab, pitch)
        second = _token_copy(ys_ref, buf_ref, sem.at[dst_slot], idx_ref[0, 0, tm + r],
                             (2 * dst_slot + 1) * tm + r, nslab, pitch)
        return first, second

    def issue(idx_ref, dst_slot):
        def body(r, carry):
            first, second = copies(idx_ref, dst_slot, r)
            first.start()
            second.start()
            return carry

        lax.fori_loop(0, tm, body, 0, unroll=ISSUE_UNROLL)

    @pl.when(i == 0)
    def _():
        issue(cur_ref, 0)

    @pl.when(i + 1 < nb)
    def _():
        issue(nxt_ref, 1 - slot)

    def drain(r, carry):
        first, second = copies(cur_ref, slot, r)
        first.wait()
        second.wait()
        return carry

    lax.fori_loop(0, tm, drain, 0, unroll=WAIT_UNROLL)
    gw = gw_ref[...]
    y0_lo, y0_hi = _load_slabs(buf_ref, (2 * slot) * (tm * pitch), 0, nslab, tm, pitch)
    y1_lo, y1_hi = _load_slabs(buf_ref, (2 * slot + 1) * (tm * pitch), 0, nslab, tm, pitch)
    y = jnp.concatenate([gw[:, 0:1] * y0_lo + gw[:, 1:2] * y1_lo,
                         gw[:, 0:1] * y0_hi + gw[:, 1:2] * y1_hi], axis=1)
    out = _layer_norm(ALPHA * x_ref[...] + y, g_ref[...], b_ref[...])
    xo_ref[...] = out
    xb_ref[...] = out.astype(BF16)


def _combine_ln(ys, pos, gate_w, x, g, b, *, tm=256):
    t, d = x.shape
    tm = _pick(t, tm)
    nb = t // tm
    nslab = d // (2 * LANES)
    pitch = _slab_pitch(nslab)
    pos_tiles = pos.reshape(2, nb, tm).transpose(1, 0, 2).reshape(nb, 1, 2 * tm)
    row = lambda i: (i, 0)
    const = lambda i: (0, 0)
    return pl.pallas_call(
        functools.partial(_combine_ln_kernel, tm=tm, nb=nb, nslab=nslab, pitch=pitch),
        grid=(nb,),
        in_specs=[
            pl.BlockSpec((1, 1, 2 * tm), lambda i: (i, 0, 0), memory_space=pltpu.SMEM),
            pl.BlockSpec((1, 1, 2 * tm), lambda i: (jnp.minimum(i + 1, nb - 1), 0, 0), memory_space=pltpu.SMEM),
            pl.BlockSpec(memory_space=pl.ANY),
            pl.BlockSpec((tm, 2), row),
            pl.BlockSpec((tm, d), row),
            pl.BlockSpec((1, d), const),
            pl.BlockSpec((1, d), const),
        ],
        out_specs=[pl.BlockSpec((tm, d), row), pl.BlockSpec((tm, d), row)],
        out_shape=[jax.ShapeDtypeStruct((t, d), F32), jax.ShapeDtypeStruct((t, d), BF16)],
        scratch_shapes=[pltpu.VMEM((4 * tm * pitch, LANES), U32), pltpu.SemaphoreType.DMA((2,))],
        compiler_params=_cparams("arbitrary"),
        name="combine_ln",
    )(pos_tiles, pos_tiles, ys, gate_w, x, g, b)


def _ple_kernel(a_ref, wg_ref, bg_ref, p_ref, wp_ref, x_ref, xo_ref, xb_ref):
    gate = _sigmoid(jnp.dot(a_ref[...], wg_ref[...], preferred_element_type=F32) + bg_ref[...])
    proj = jnp.dot(p_ref[...], wp_ref[...], preferred_element_type=F32)
    out = x_ref[...] + gate * proj
    xo_ref[...] = out
    xb_ref[...] = out.astype(BF16)


def _per_layer_input(xb, x, p, w_gate, b_gate, w_proj, *, tm=1024, tn=512):
    m, d = x.shape
    pd = p.shape[1]
    tm, tn = _pick(m, tm), _pick(d, tn)
    tile = lambda i, j: (i, j)
    return pl.pallas_call(
        _ple_kernel,
        grid=(m // tm, d // tn),
        in_specs=[
            pl.BlockSpec((tm, d), lambda i, j: (i, 0)),
            pl.BlockSpec((d, tn), lambda i, j: (0, j)),
            pl.BlockSpec((1, tn), lambda i, j: (0, j)),
            pl.BlockSpec((tm, pd), lambda i, j: (i, 0)),
            pl.BlockSpec((pd, tn), lambda i, j: (0, j)),
            pl.BlockSpec((tm, tn), tile),
        ],
        out_specs=[pl.BlockSpec((tm, tn), tile), pl.BlockSpec((tm, tn), tile)],
        out_shape=[jax.ShapeDtypeStruct((m, d), F32), jax.ShapeDtypeStruct((m, d), BF16)],
        compiler_params=_cparams("parallel", "parallel"),
        name="per_layer_input",
    )(xb, w_gate, b_gate, p, w_proj, x)


def _rope_table_kernel(pos_ref, cos_ref, sin_ref):
    half = QK_ROPE_DIM // 2
    lane = lax.broadcasted_iota(jnp.int32, (1, LANES), 1)
    freq = (lane % half).astype(F32)
    inv_freq = 1.0 / (ROPE_THETA ** (freq * (2.0 / QK_ROPE_DIM)))
    ang = pos_ref[...].astype(F32) * inv_freq
    sign = jnp.where((lane // half) % 2 == 0, -1.0, 1.0)
    cos_ref[...] = jnp.cos(ang)
    sin_ref[...] = jnp.sin(ang) * sign


def _rope_tables(positions, *, tm=2048):
    t = positions.size
    tm = _pick(t, tm)
    return pl.pallas_call(
        _rope_table_kernel,
        grid=(t // tm,),
        in_specs=[pl.BlockSpec((tm, 1), lambda i: (i, 0))],
        out_specs=[pl.BlockSpec((tm, LANES), lambda i: (i, 0))] * 2,
        out_shape=[jax.ShapeDtypeStruct((t, LANES), F32)] * 2,
        compiler_params=_cparams("parallel"),
        name="rope_tables",
    )(positions.reshape(t, 1))


def _latent_kernel(a_ref, w_ref, gq_ref, gkv_ref, cos_ref, sin_ref, cq_ref, ckv_ref, kr_ref, *, qr):
    c = jnp.dot(a_ref[...], w_ref[...], preferred_element_type=F32)
    cq = c[:, :qr]
    cq_ref[...] = (cq * lax.rsqrt(jnp.mean(cq * cq, axis=-1, keepdims=True) + RMS_EPS)
                   * gq_ref[...]).astype(BF16)
    ckv = c[:, qr:qr + KV_LORA_RANK]
    ckv_ref[...] = (ckv * lax.rsqrt(jnp.mean(ckv * ckv, axis=-1, keepdims=True) + RMS_EPS)
                    * gkv_ref[...]).astype(BF16)
    kpe = c[:, qr + KV_LORA_RANK:]
    cos = cos_ref[...]
    sin = sin_ref[...]
    rot = kpe[:, :QK_ROPE_DIM] * cos[:, :QK_ROPE_DIM] + kpe[:, QK_ROPE_DIM:] * sin[:, :QK_ROPE_DIM]
    kr_ref[...] = rot.astype(BF16)


def _latents(xb, w_cat, gq, gkv, cos_t, sin_t, *, qr, tm=512):
    m, kd = xb.shape
    n = w_cat.shape[1]
    tm = _pick(m, tm)
    row = lambda i: (i, 0)
    const = lambda i: (0, 0)
    return pl.pallas_call(
        functools.partial(_latent_kernel, qr=qr),
        grid=(m // tm,),
        in_specs=[
            pl.BlockSpec((tm, kd), row),
            pl.BlockSpec((kd, n), const),
            pl.BlockSpec((1, qr), const),
            pl.BlockSpec((1, KV_LORA_RANK), const),
            pl.BlockSpec((tm, LANES), row),
            pl.BlockSpec((tm, LANES), row),
        ],
        out_specs=[
            pl.BlockSpec((tm, qr), row),
            pl.BlockSpec((tm, KV_LORA_RANK), row),
            pl.BlockSpec((tm, QK_ROPE_DIM), row),
        ],
        out_shape=[
            jax.ShapeDtypeStruct((m, qr), BF16),
            jax.ShapeDtypeStruct((m, KV_LORA_RANK), BF16),
            jax.ShapeDtypeStruct((m, QK_ROPE_DIM), BF16),
        ],
        compiler_params=_cparams("parallel"),
        name="latents",
    )(xb, w_cat, gq, gkv, cos_t, sin_t)


def _rope_q_kernel(a_ref, w_ref, ws_ref, cos_ref, sin_ref, o_ref, *, scale, reps):
    a = a_ref[...]
    q = jnp.dot(a, w_ref[...], preferred_element_type=F32)
    qs = jnp.dot(a, ws_ref[...], preferred_element_type=F32)
    cos = jnp.tile(cos_ref[...], (1, reps))
    sin = jnp.tile(sin_ref[...], (1, reps))
    o_ref[...] = ((q * cos + qs * sin) * scale).astype(o_ref.dtype)


def _rope_q(cq, w_r, w_r_swapped, cos_t, sin_t, *, scale, tm=1024, tn=1024):
    m, kd = cq.shape
    n = w_r.shape[1]
    tm, tn = _pick(m, tm), _pick(n, tn)
    return pl.pallas_call(
        functools.partial(_rope_q_kernel, scale=scale, reps=tn // LANES),
        grid=(m // tm, n // tn),
        in_specs=[
            pl.BlockSpec((tm, kd), lambda i, j: (i, 0)),
            pl.BlockSpec((kd, tn), lambda i, j: (0, j)),
            pl.BlockSpec((kd, tn), lambda i, j: (0, j)),
            pl.BlockSpec((tm, LANES), lambda i, j: (i, 0)),
            pl.BlockSpec((tm, LANES), lambda i, j: (i, 0)),
        ],
        out_specs=pl.BlockSpec((tm, tn), lambda i, j: (i, j)),
        out_shape=jax.ShapeDtypeStruct((m, n), BF16),
        compiler_params=_cparams("parallel", "parallel"),
        name="rope_q",
    )(cq, w_r, w_r_swapped, cos_t, sin_t)


ATTN_BLOCK = 256


def _attn_kernel(qn_ref, qr_ref, kv_ref, kr_ref, o_ref, qc_ref, kc_ref, s_ref, p_ref, *, seq):
    blk = ATTN_BLOCK
    nt = (((1,), (1,)), ((), ()))
    qk = QK_NOPE_DIM + QK_ROPE_DIM
    hw = QK_NOPE_DIM + V_HEAD_DIM
    row = lax.broadcasted_iota(jnp.int32, (blk, blk), 0)
    col = lax.broadcasted_iota(jnp.int32, (blk, blk), 1)
    for h in range(2):
        qc_ref[:, :QK_NOPE_DIM] = qn_ref[:, h * QK_NOPE_DIM:(h + 1) * QK_NOPE_DIM]
        qc_ref[:, QK_NOPE_DIM:qk] = qr_ref[:, h * QK_ROPE_DIM:(h + 1) * QK_ROPE_DIM]
        kc_ref[:, :QK_NOPE_DIM] = kv_ref[:, h * hw:h * hw + QK_NOPE_DIM]
        kc_ref[:, QK_NOPE_DIM:qk] = kr_ref[...]
        for i in range(seq // blk):
            par = i % 2
            q = qc_ref[i * blk:(i + 1) * blk, :]
            m = None
            for j in range(i + 1):
                s = lax.dot_general(q, kc_ref[j * blk:(j + 1) * blk, :], nt, preferred_element_type=F32)
                if j == i:
                    s = jnp.where(col <= row, s, -jnp.inf)
                s_ref[par, :, j * blk:(j + 1) * blk] = s
                mj = jnp.max(s, axis=-1, keepdims=True)
                m = mj if m is None else jnp.maximum(m, mj)
            l = None
            for j in range(i + 1):
                p = jnp.exp(s_ref[par, :, j * blk:(j + 1) * blk] - m)
                lj = jnp.sum(p, axis=-1, keepdims=True)
                l = lj if l is None else l + lj
                p_ref[par, :, j * blk:(j + 1) * blk] = p.astype(BF16)
            w = (i + 1) * blk
            v = kv_ref[0:w, h * hw + QK_NOPE_DIM:(h + 1) * hw]
            acc = jnp.dot(p_ref[par, :, 0:w], v, preferred_element_type=F32)
            o_ref[i * blk:(i + 1) * blk, h * V_HEAD_DIM:(h + 1) * V_HEAD_DIM] = (acc / l).astype(o_ref.dtype)


def _attention(qn, qr, kv, kr, *, batch, seq):
    t = qn.shape[0]
    hp = N_HEADS // 2
    kvw = 2 * (QK_NOPE_DIM + V_HEAD_DIM)
    qk = QK_NOPE_DIM + QK_ROPE_DIM
    return pl.pallas_call(
        functools.partial(_attn_kernel, seq=seq),
        grid=(batch, hp),
        in_specs=[
            pl.BlockSpec((seq, 2 * QK_NOPE_DIM), lambda b, g: (b, g)),
            pl.BlockSpec((seq, 2 * QK_ROPE_DIM), lambda b, g: (b, g)),
            pl.BlockSpec((seq, kvw), lambda b, g: (b, g)),
            pl.BlockSpec((seq, QK_ROPE_DIM), lambda b, g: (b, 0)),
        ],
        out_specs=pl.BlockSpec((seq, 2 * V_HEAD_DIM), lambda b, g: (b, g)),
        out_shape=jax.ShapeDtypeStruct((t, N_HEADS * V_HEAD_DIM), BF16),
        scratch_shapes=[
            pltpu.VMEM((seq, qk), BF16),
            pltpu.VMEM((seq, qk), BF16),
            pltpu.VMEM((2, ATTN_BLOCK, seq), F32),
            pltpu.VMEM((2, ATTN_BLOCK, seq), BF16),
        ],
        compiler_params=_cparams("parallel", "parallel"),
        name="attention",
    )(qn, qr, kv, kr)


def _split_hi_lo(w):
    hi = w.astype(BF16)
    lo = (w - hi.astype(F32)).astype(BF16)
    return hi, lo


def _swap_halves(w, width):
    k, n = w.shape
    return w.reshape(k, n // width, 2, width // 2)[:, :, ::-1, :].reshape(k, n)


def _moe_layer(x, x_slabs, logits_t, router_b, w_in, w_down, layer, g, b):
    t, d = x.shape
    tm = MOE_ROW_TILE
    experts, gate_w, rank, counts = _route(logits_t, router_b)

    cnt = counts[:, 0].astype(jnp.int32)
    tiles_per_expert = (cnt + tm - 1) // tm
    eid = jnp.arange(N_EXPERTS, dtype=jnp.int32)
    tile_end = jnp.sum(jnp.where(eid[None, :] <= eid[:, None], tiles_per_expert[None, :], 0), axis=1)
    row_off = (tile_end - tiles_per_expert) * tm
    n_tiles = (2 * t) // tm + N_EXPERTS
    n_rows = n_tiles * tm
    tile_id = jnp.arange(n_tiles, dtype=jnp.int32)
    n_used = tile_end[-1].astype(jnp.int32)
    tile_valid = jnp.concatenate([(tile_id < n_used).astype(jnp.int32), n_used[None]])
    last_tile = jnp.maximum(n_used - 1, 0)
    before = tile_end[None, :] <= jnp.minimum(tile_id, last_tile)[:, None]
    tile_expert = jnp.minimum(jnp.sum(before.astype(jnp.int32), axis=1), N_EXPERTS - 1)
    prev_expert = jnp.concatenate([jnp.full((1,), -1, jnp.int32), tile_expert[:-1]])
    run_first = jnp.logical_and(tile_id < n_used, tile_expert != prev_expert).astype(jnp.int32)
    later = jnp.logical_and(eid[None, :] > eid[:, None], tiles_per_expert[None, :] > 0)
    next_nonempty = jnp.min(jnp.where(later, eid[None, :], N_EXPERTS), axis=1)
    next_nonempty = jnp.where(next_nonempty < N_EXPERTS, next_nonempty, -1)
    run_next = jnp.sum(jnp.where(tile_expert[:, None] == eid, next_nonempty, 0), axis=1).astype(jnp.int32)
    plan = (tile_expert, tile_valid, run_first, run_next)
    onehot = experts[:, :, None] == eid
    pos = jnp.sum(jnp.where(onehot, row_off, 0), axis=-1) + rank
    tok = jnp.broadcast_to(jnp.arange(t, dtype=jnp.int32)[None, :], (2, t))
    src_tok = jnp.zeros((n_rows,), jnp.int32).at[pos.reshape(-1)].set(tok.reshape(-1))

    hmid = _ffn_up(x_slabs, src_tok, w_in, layer, plan, tm=tm)
    ys = _ffn_down(hmid, w_down, layer, plan, tm=tm)
    return _combine_ln(ys, pos, gate_w.T, x, g, b)


def kernel(x, p, positions, conv_w_in, conv_b_in, conv_w_dw, conv_b_dw, conv_ln_g, conv_ln_b, conv_w_out, conv_b_out, kv_w_down, kv_norm_g, kv_w_up, q_w_down, q_norm_g, q_w_up, attn_w_out, router_w, router_b, moe_w_in, moe_w_down, ln1_g, ln1_b, ln2_g, ln2_b, ple_w_gate, ple_b_gate, ple_w_proj):
    batch, seq, d = x.shape
    t = batch * seq
    row = lambda v: v.reshape(1, -1).astype(F32)

    x0 = x.reshape(t, d)
    rw_hi, rw_lo = _split_hi_lo(router_w.T)
    pb = p.reshape(DEPTH, t, -1).astype(BF16)

    glu = _glu_matmul(x0.astype(BF16), conv_w_in[0].astype(BF16), row(conv_b_in[0]))
    hn = _conv_ln_swish(glu, conv_w_dw[0].reshape(CONV_KERNEL, d), row(conv_b_dw[0]),
                        row(conv_ln_g[0]), row(conv_ln_b[0]), batch=batch, seq=seq)
    mix = _matmul(hn, conv_w_out[0].astype(BF16), row(conv_b_out[0]))
    x1, x1s, lg = _ln_router(mix, x0, row(ln1_g[0]), row(ln1_b[0]), rw_hi, rw_lo)
    x2, x2b = _moe_layer(x1, x1s, lg, router_b, moe_w_in, moe_w_down, 0, row(ln2_g[0]), row(ln2_b[0]))
    x3, x3b = _per_layer_input(x2b, x2, pb[0], ple_w_gate[0].astype(BF16), row(ple_b_gate[0]),
                               ple_w_proj[0].astype(BF16))

    cos_t, sin_t = _rope_tables(positions)
    qr_rank = q_w_down.shape[2]
    w_kpe = kv_w_down[:, KV_LORA_RANK:]
    w_cat = jnp.concatenate([q_w_down[0], kv_w_down[:, :KV_LORA_RANK], w_kpe,
                             _swap_halves(w_kpe, QK_ROPE_DIM)], axis=1).astype(BF16)
    cq, ckv, k_rot = _latents(x3b, w_cat, row(q_norm_g[0]), row(kv_norm_g), cos_t, sin_t, qr=qr_rank)
    kv = _matmul(ckv, kv_w_up.reshape(KV_LORA_RANK, -1).astype(BF16))

    w_qn = q_w_up[0][:, :, :QK_NOPE_DIM].reshape(qr_rank, -1).astype(BF16)
    w_qr = q_w_up[0][:, :, QK_NOPE_DIM:].reshape(qr_rank, -1)
    qn = _matmul(cq, w_qn, scale=SOFTMAX_SCALE)
    q_rot = _rope_q(cq, w_qr.astype(BF16), _swap_halves(w_qr, QK_ROPE_DIM).astype(BF16), cos_t, sin_t,
                    scale=SOFTMAX_SCALE)
    o = _attention(qn, q_rot, kv, k_rot, batch=batch, seq=seq)
    mix = _matmul(o, attn_w_out[0].astype(BF16))
    x4, x4s, lg = _ln_router(mix, x3, row(ln1_g[1]), row(ln1_b[1]), rw_hi, rw_lo)
    x5, x5b = _moe_layer(x4, x4s, lg, router_b, moe_w_in, moe_w_down, 1, row(ln2_g[1]), row(ln2_b[1]))
    x6, _ = _per_layer_input(x5b, x5, pb[1], ple_w_gate[1].astype(BF16), row(ple_b_gate[1]),
                             ple_w_proj[1].astype(BF16))
    return x6.reshape(batch, seq, d)
```
